```python
import math
import jax, jax.numpy as jnp
from jax import lax
import numpy as np

D_MODEL = 1024
BATCH = 32
SEQ = 2048
DEPTH = 4

GRID_W = 64
CTX_LEN = 256
N_MIXERS = 2
HEAD_DIM = 64
N_HEADS = D_MODEL // HEAD_DIM
A_KV_HEADS = 4
A_GROUPS = N_HEADS // A_KV_HEADS
A_WINDOW = 128
A_BLOCK = 128
A_INNER = N_HEADS * HEAD_DIM
A_KV_DIM = A_KV_HEADS * HEAD_DIM
A_IN_DIM = 2 * A_INNER + 2 * A_KV_DIM
B_KH_MAX = 8
B_KW = 16
B_QW = 16
B_REG_W = 2 * B_KW
B_INNER = N_HEADS * HEAD_DIM
B_IN_DIM = 4 * B_INNER
ROPE_BASE = 10000.0
LN_EPS = 1e-5
NEG_INF = -1e30
DEEPNORM_ALPHA = (2.0 * DEPTH) ** 0.25
DEEPNORM_BETA = (8.0 * DEPTH) ** -0.25
N_LAYERS_A = (DEPTH + 1) // 2
N_LAYERS_B = DEPTH // 2

kernel_name = "hybrid_window_gqa_neighbourhood_diffusion_trunk"


def layer_norm(x, g, b):
    xf = x.astype(jnp.float32)
    mu = jnp.mean(xf, axis=-1, keepdims=True)
    var = jnp.mean(jnp.square(xf - mu), axis=-1, keepdims=True)
    y = (xf - mu) * lax.rsqrt(var + LN_EPS)
    return (y * g.astype(jnp.float32) + b.astype(jnp.float32)).astype(x.dtype)


def axial_rope_tables(n_tok):
    t = jnp.arange(n_tok)
    row = (t // GRID_W).astype(jnp.float32)
    col = (t % GRID_W).astype(jnp.float32)
    n_freq = HEAD_DIM // 4
    inv = ROPE_BASE ** (-jnp.arange(n_freq, dtype=jnp.float32) / n_freq)
    ang_r = row[:, None] * inv[None]
    ang_c = col[:, None] * inv[None]
    return (jnp.cos(ang_r), jnp.sin(ang_r), jnp.cos(ang_c), jnp.sin(ang_c))


def _rotate(x, cos, sin):
    a, b = jnp.split(x, 2, axis=-1)
    cos = cos[None, :, None, :].astype(x.dtype)
    sin = sin[None, :, None, :].astype(x.dtype)
    return jnp.concatenate([a * cos - b * sin, b * cos + a * sin], axis=-1)


def apply_axial_rope(x, rope):
    cos_r, sin_r, cos_c, sin_c = rope
    xr, xc = jnp.split(x, 2, axis=-1)
    return jnp.concatenate([_rotate(xr, cos_r, sin_r), _rotate(xc, cos_c, sin_c)], axis=-1)


def ctx_attention(q, k, v, sink):
    B, C = q.shape[0], q.shape[1]
    s = jnp.einsum('bqkgd,bckd->bkgqc', q, k).astype(jnp.float32) * (HEAD_DIM ** -0.5)
    if sink is not None:
        s_sink = jnp.broadcast_to(sink[None, :, :, None, None], s.shape[:-1] + (1,))
        p = jax.nn.softmax(jnp.concatenate([s, s_sink], axis=-1), axis=-1)[..., :C]
    else:
        p = jax.nn.softmax(s, axis=-1)
    o = jnp.einsum('bkgqc,bckd->bqkgd', p.astype(v.dtype), v)
    return o.reshape(B, C, -1)


def _split_a(p):
    return jnp.split(p, [A_INNER, A_INNER + A_KV_DIM, A_INNER + 2 * A_KV_DIM], axis=-1)


def mixer_a(u, uc, w_in, w_out, sink, rope, ctx_out):
    B, S, _ = u.shape
    C = uc.shape[1]
    scale = HEAD_DIM ** -0.5
    q, k, v, g = _split_a(u @ w_in)
    q = apply_axial_rope(q.reshape(B, S, N_HEADS, HEAD_DIM), rope).reshape(B, S, A_KV_HEADS, A_GROUPS, HEAD_DIM)
    k = apply_axial_rope(k.reshape(B, S, A_KV_HEADS, HEAD_DIM), rope)
    v = v.reshape(B, S, A_KV_HEADS, HEAD_DIM)
    qc, kc, vc, gc = _split_a(uc @ w_in)
    kc = kc.reshape(B, C, A_KV_HEADS, HEAD_DIM)
    vc = vc.reshape(B, C, A_KV_HEADS, HEAD_DIM)
    sink_l = sink.reshape(A_KV_HEADS, A_GROUPS).astype(jnp.float32)
    span = A_BLOCK + 2 * A_WINDOW
    pad = ((0, 0), (A_WINDOW, A_WINDOW), (0, 0), (0, 0))
    kp = jnp.pad(k, pad)
    vp = jnp.pad(v, pad)

    def block(bi):
        start = bi * A_BLOCK
        qb = lax.dynamic_slice_in_dim(q, start, A_BLOCK, axis=1)
        kb = lax.dynamic_slice_in_dim(kp, start, span, axis=1)
        vb = lax.dynamic_slice_in_dim(vp, start, span, axis=1)
        qi = start + jnp.arange(A_BLOCK)
        kj = start - A_WINDOW + jnp.arange(span)
        mask = (jnp.abs(qi[:, None] - kj[None, :]) <= A_WINDOW) & (kj >= 0)[None, :] & (kj < S)[None, :]
        s_lat = jnp.einsum('bqkgd,bjkd->bkgqj', qb, kb).astype(jnp.float32) * scale
        s_lat = jnp.where(mask, s_lat, NEG_INF)
        s_ctx = jnp.einsum('bqkgd,bckd->bkgqc', qb, kc).astype(jnp.float32) * scale
        s_sink = jnp.broadcast_to(sink_l[None, :, :, None, None], s_lat.shape[:-1] + (1,))
        p = jax.nn.softmax(jnp.concatenate([s_lat, s_ctx, s_sink], axis=-1), axis=-1)
        p_lat = p[..., :span].astype(v.dtype)
        p_ctx = p[..., span:span + C].astype(v.dtype)
        o = jnp.einsum('bkgqj,bjkd->bqkgd', p_lat, vb) + jnp.einsum('bkgqc,bckd->bqkgd', p_ctx, vc)
        return o.reshape(B, A_BLOCK, A_INNER)

    o = lax.map(block, jnp.arange(S // A_BLOCK))
    o = jnp.transpose(o, (1, 0, 2, 3)).reshape(B, S, A_INNER)
    y = (o * jax.nn.silu(g)) @ w_out
    if not ctx_out:
        return y, None
    oc = ctx_attention(qc.reshape(B, C, A_KV_HEADS, A_GROUPS, HEAD_DIM), kc, vc, sink_l)
    yc = (oc * jax.nn.silu(gc)) @ w_out
    return y, yc


def mixer_b(u, uc, w_in, w_out, rel_bias, ctx_out):
    B, S, _ = u.shape
    C = uc.shape[1]
    rows = S // GRID_W
    kh = min(B_KH_MAX, rows)
    scale = HEAD_DIM ** -0.5
    q, k, v, g = jnp.split(u @ w_in, 4, axis=-1)
    q = q.reshape(B, rows, GRID_W, N_HEADS, HEAD_DIM)
    k = k.reshape(B, rows, GRID_W, N_HEADS, HEAD_DIM)
    v = v.reshape(B, rows, GRID_W, N_HEADS, HEAD_DIM)
    qc, kc, vc, gc = jnp.split(uc @ w_in, 4, axis=-1)
    kc = kc.reshape(B, C, N_HEADS, HEAD_DIM)
    vc = vc.reshape(B, C, N_HEADS, HEAD_DIM)

    n_cb = GRID_W // B_QW
    c0 = np.arange(n_cb) * B_QW
    cs_blk = np.clip(c0 - B_KW // 2, 0, GRID_W - B_REG_W)
    key_col = cs_blk[:, None] + np.arange(B_REG_W)
    q_col = c0[:, None] + np.arange(B_QW)
    cs_q = np.clip(q_col - B_KW // 2, 0, GRID_W - B_KW)
    col_mask = (key_col[:, None, :] >= cs_q[:, :, None]) & (key_col[:, None, :] < cs_q[:, :, None] + B_KW)
    col_idx = np.clip(key_col[:, None, :] - q_col[:, :, None] + (B_KW - 1), 0, 2 * B_KW - 2)
    key_col = jnp.asarray(key_col, dtype=jnp.int32)
    col_idx = jnp.asarray(col_idx, dtype=jnp.int32)
    col_mask = jnp.asarray(col_mask)[:, :, None, :]
    bias_tab = rel_bias.astype(jnp.float32)

    def row_block(r):
        rs = jnp.clip(r - kh // 2, 0, rows - kh)
        q_r = lax.dynamic_index_in_dim(q, r, axis=1, keepdims=False).reshape(B, n_cb, B_QW, N_HEADS, HEAD_DIM)
        k_rows = lax.dynamic_slice_in_dim(k, rs, kh, axis=1)
        v_rows = lax.dynamic_slice_in_dim(v, rs, kh, axis=1)
        k_reg = k_rows[:, :, key_col]
        v_reg = v_rows[:, :, key_col]
        s = jnp.einsum('bnqhd,brnjhd->bhnqrj', q_r, k_reg).astype(jnp.float32) * scale
        row_idx = rs + jnp.arange(kh) - r + (B_KH_MAX - 1)
        bias = bias_tab[:, row_idx[:, None, None, None], col_idx[None]]
        s = s + jnp.transpose(bias, (0, 2, 3, 1, 4))[None]
        s = jnp.where(col_mask, s, NEG_INF).reshape(B, N_HEADS, n_cb, B_QW, kh * B_REG_W)
        s_ctx = jnp.einsum('bnqhd,bchd->bhnqc', q_r, kc).astype(jnp.float32) * scale
        p = jax.nn.softmax(jnp.concatenate([s, s_ctx], axis=-1), axis=-1)
        p_lat = p[..., :kh * B_REG_W].reshape(B, N_HEADS, n_cb, B_QW, kh, B_REG_W).astype(v.dtype)
        p_ctx = p[..., kh * B_REG_W:].astype(v.dtype)
        o = jnp.einsum('bhnqrj,brnjhd->bnqhd', p_lat, v_reg) + jnp.einsum('bhnqc,bchd->bnqhd', p_ctx, vc)
        return o.reshape(B, GRID_W, B_INNER)

    o = lax.map(row_block, jnp.arange(rows))
    o = jnp.transpose(o, (1, 0, 2, 3)).reshape(B, S, B_INNER)
    y = (o * jax.nn.silu(g)) @ w_out
    if not ctx_out:
        return y, None
    oc = ctx_attention(qc.reshape(B, C, N_HEADS, 1, HEAD_DIM), kc[:, :, :, :], vc, None)
    yc = (oc * jax.nn.silu(gc)) @ w_out
    return y, yc


def setup_inputs(seed: int = 0) -> dict:
    key = jax.random.key(seed)
    ks = jax.random.split(key, 14)
    f32 = jnp.float32
    d = D_MODEL
    return {
        "x": jax.random.normal(ks[0], (BATCH, SEQ, d), f32),
        "c": jax.random.normal(ks[1], (BATCH, d), f32),
        "ctx": jax.random.normal(ks[2], (BATCH, CTX_LEN, d), f32),
        "c_ctx": jax.random.normal(ks[3], (d,), f32),
        "w_ada": jax.random.normal(ks[4], (DEPTH, d, 3 * d), f32) * d ** -0.5,
        "b_ada": 0.01 * jax.random.normal(ks[5], (DEPTH, 3 * d), f32),
        "ln_g": 1.0 + 0.05 * jax.random.normal(ks[6], (DEPTH, d), f32),
        "ln_b": 0.02 * jax.random.normal(ks[7], (DEPTH, d), f32),
        "a_w_in": jax.random.normal(ks[8], (N_LAYERS_A, d, A_IN_DIM), f32) * d ** -0.5,
        "a_w_out": jax.random.normal(ks[9], (N_LAYERS_A, A_INNER, d), f32) * (A_INNER ** -0.5 * DEEPNORM_BETA),
        "a_sink": 0.5 * jax.random.normal(ks[10], (N_LAYERS_A, N_HEADS), f32),
        "b_w_in": jax.random.normal(ks[11], (N_LAYERS_B, d, B_IN_DIM), f32) * d ** -0.5,
        "b_w_out": jax.random.normal(ks[12], (N_LAYERS_B, B_INNER, d), f32) * (B_INNER ** -0.5 * DEEPNORM_BETA),
        "b_rel_bias": 0.5 * jax.random.normal(ks[13], (N_LAYERS_B, N_HEADS, 2 * B_KH_MAX - 1, 2 * B_KW - 1), f32),
    }


def reference(x, c, ctx, c_ctx, w_ada, b_ada, ln_g, ln_b, a_w_in, a_w_out, a_sink, b_w_in, b_w_out, b_rel_bias):
    S = x.shape[1]
    rope = axial_rope_tables(S)
    silu_c = jax.nn.silu(c)
    silu_cc = jax.nn.silu(c_ctx)
    h, hc = x, ctx
    for i in range(DEPTH):
        shift, scale, gate = jnp.split(silu_c @ w_ada[i] + b_ada[i], 3, axis=-1)
        shift_c, scale_c, gate_c = jnp.split(silu_cc @ w_ada[i] + b_ada[i], 3, axis=-1)
        ctx_out = i < DEPTH - 1
        u = h * (1.0 + scale[:, None, :]) + shift[:, None, :]
        uc = hc * (1.0 + scale_c) + shift_c
        j = i // N_MIXERS
        if i % N_MIXERS == 0:
            y, yc = mixer_a(u, uc, a_w_in[j], a_w_out[j], a_sink[j], rope, ctx_out)
        else:
            y, yc = mixer_b(u, uc, b_w_in[j], b_w_out[j], b_rel_bias[j], ctx_out)
        h = layer_norm(DEEPNORM_ALPHA * h + gate[:, None, :] * y, ln_g[i], ln_b[i])
        if ctx_out:
            hc = layer_norm(DEEPNORM_ALPHA * hc + gate_c * yc, ln_g[i], ln_b[i])
    return h
```

```python
import functools

import numpy as np
import jax
import jax.numpy as jnp
from jax import lax
from jax.experimental import pallas as pl
from jax.experimental.pallas import tpu as pltpu

D_MODEL = 1024
DEPTH = 4
GRID_W = 64
HEAD_DIM = 64
N_HEADS = 16
A_KV_HEADS = 4
A_GROUPS = 4
A_WINDOW = 128
A_KV_DIM = A_KV_HEADS * HEAD_DIM
B_KH = 8
B_KW = 16
ROPE_BASE = 10000.0
LN_EPS = 1e-5
NEG_INF = -1e30
DEEPNORM_ALPHA = (2.0 * DEPTH) ** 0.25
Q_SCALE = HEAD_DIM ** -0.5

LANES = 128
BLK = 256
SLOTS = BLK // HEAD_DIM
N_BLKS = D_MODEL // BLK
MOD_ROWS = 40
VMEM_LIMIT = 56 * 1024 * 1024

F32 = jnp.float32
BF16 = jnp.bfloat16


def _silu(x):
    return x / (1.0 + jnp.exp(-x))


def _slot_mask(slot):
    lane = lax.broadcasted_iota(jnp.int32, (1, BLK), 1)
    return (lane // HEAD_DIM) == slot


def _ada_kernel(c_ref, w_ref, b_ref, o_ref):
    sc = _silu(c_ref[...])
    o_ref[...] = jnp.dot(sc, w_ref[...], preferred_element_type=F32,
                         precision=lax.Precision.HIGHEST) + b_ref[...]


def _ada(cc, w_ada, b_ada):
    out = pl.pallas_call(
        _ada_kernel,
        grid=(DEPTH, 3),
        in_specs=[
            pl.BlockSpec((MOD_ROWS, D_MODEL), lambda d, j: (0, 0)),
            pl.BlockSpec((None, D_MODEL, D_MODEL), lambda d, j: (d, 0, j)),
            pl.BlockSpec((None, None, 1, D_MODEL), lambda d, j: (d, j, 0, 0)),
        ],
        out_specs=pl.BlockSpec((None, None, MOD_ROWS, D_MODEL), lambda d, j: (d, j, 0, 0)),
        out_shape=jax.ShapeDtypeStruct((DEPTH, 3, MOD_ROWS, D_MODEL), F32),
        compiler_params=pltpu.CompilerParams(
            dimension_semantics=("arbitrary", "arbitrary"), vmem_limit_bytes=VMEM_LIMIT),
        name="ada_mod",
    )(cc, w_ada, b_ada.reshape(DEPTH, 3, 1, D_MODEL))
    return out.reshape(DEPTH, 3, MOD_ROWS, 1, D_MODEL)


def _inproj_kernel(*refs, n_out, rope_chunks, chunk):
    if rope_chunks:
        x_ref, shift_ref, scale_ref, w_ref, cos_ref, sina_ref, sinb_ref, o_ref = refs
    else:
        x_ref, shift_ref, scale_ref, w_ref, o_ref = refs
    u = (x_ref[...] * (1.0 + scale_ref[...]) + shift_ref[...]).astype(BF16)
    for ci in range(n_out // chunk):
        c0 = ci * chunk
        acc = jnp.dot(u, w_ref[:, c0:c0 + chunk], preferred_element_type=F32)
        if ci in rope_chunks:
            cos, sina, sinb = cos_ref[...], sina_ref[...], sinb_ref[...]
            parts = []
            for l0 in range(0, chunk, LANES):
                a = acc[:, l0:l0 + LANES]
                parts.append(a * cos + pltpu.roll(a, LANES - 16, 1) * sina + pltpu.roll(a, 16, 1) * sinb)
            acc = jnp.concatenate(parts, axis=1)
        o_ref[:, c0:c0 + chunk] = acc.astype(BF16)


def _inproj(h, mod, layer, w, *, tm, mod_row, rope_tabs=None, rope_chunks=()):
    bsz, seq, _ = h.shape
    n_out = w.shape[1]
    chunk = BLK
    if mod_row is None:
        row = lambda b: b
    else:
        row = lambda b: mod_row
    in_specs = [
        pl.BlockSpec((None, tm, D_MODEL), lambda b, s: (b, s, 0)),
        pl.BlockSpec((None, None, None, 1, D_MODEL), lambda b, s: (layer, 0, row(b), 0, 0)),
        pl.BlockSpec((None, None, None, 1, D_MODEL), lambda b, s: (layer, 1, row(b), 0, 0)),
        pl.BlockSpec((D_MODEL, n_out), lambda b, s: (0, 0)),
    ]
    args = [h, mod, mod, w]
    if rope_chunks:
        in_specs += [pl.BlockSpec((tm, LANES), lambda b, s: (s, 0))] * 3
        args += list(rope_tabs)
    return pl.pallas_call(
        functools.partial(_inproj_kernel, n_out=n_out, rope_chunks=tuple(rope_chunks), chunk=chunk),
        grid=(bsz, seq // tm),
        in_specs=in_specs,
        out_specs=pl.BlockSpec((None, tm, n_out), lambda b, s: (b, s, 0)),
        out_shape=jax.ShapeDtypeStruct((bsz, seq, n_out), BF16),
        compiler_params=pltpu.CompilerParams(
            dimension_semantics=("arbitrary", "arbitrary"), vmem_limit_bytes=VMEM_LIMIT),
        name="inproj",
    )(*args)


def _stack_masked(q_blocks, slots):
    return jnp.concatenate(
        [jnp.where(_slot_mask(s), q, jnp.zeros_like(q)) for q, s in zip(q_blocks, slots)], axis=0)

def _attend(qs, k, v, tq, biases, sinks):
    s = lax.dot_general(qs, k, (((1,), (1,)), ((), ())), preferred_element_type=F32)
    ps, invs = [], []
    for a in range(SLOTS):
        sa = s[a * tq:(a + 1) * tq]
        if biases[a] is not None:
            sa = sa + biases[a]
        m = jnp.max(sa, axis=-1, keepdims=True)
        if sinks[a] is not None:
            m = jnp.maximum(m, sinks[a])
        p = jnp.exp(sa - m)
        l = jnp.sum(p, axis=-1, keepdims=True)
        if sinks[a] is not None:
            l = l + jnp.exp(sinks[a] - m)
        ps.append(p.astype(BF16))
        invs.append(1.0 / l)
    o = jnp.dot(jnp.concatenate(ps, axis=0), v, preferred_element_type=F32)
    return [o[a * tq:(a + 1) * tq] * invs[a] for a in range(SLOTS)]


def _gate_out_norm(o, g, h, gate, wout_ref, lng, lnb):
    gated = (o * _silu(g.astype(F32))).astype(BF16)
    y = jnp.dot(gated, wout_ref[...], preferred_element_type=F32)
    r = DEEPNORM_ALPHA * h + gate * y
    mu = jnp.mean(r, axis=-1, keepdims=True)
    rc = r - mu
    var = jnp.mean(rc * rc, axis=-1, keepdims=True)
    return rc * lax.rsqrt(var + LN_EPS) * lng + lnb


def _attn_a_kernel(sink_ref, q_ref, g_ref, kp_ref, kc_ref, kn_ref, vp_ref, vc_ref, vn_ref,
                   kx_ref, vx_ref, h_ref, gate_ref, wout_ref, lng_ref, lnb_ref, o_ref, *, tq):
    i = pl.program_id(1)
    nb = pl.num_programs(1)
    n_ctx = kx_ref.shape[0]
    k = jnp.concatenate([kp_ref[...], kc_ref[...], kn_ref[...], kx_ref[...]], axis=0)
    v = jnp.concatenate([vp_ref[...], vc_ref[...], vn_ref[...], vx_ref[...]], axis=0)
    nk = 3 * tq + n_ctx
    qi = lax.broadcasted_iota(jnp.int32, (tq, nk), 0)
    kj = lax.broadcasted_iota(jnp.int32, (tq, nk), 1)
    rel = kj - tq - qi
    ok = (jnp.abs(rel) <= A_WINDOW) | (kj >= 3 * tq)
    ok = ok & ((kj >= tq) | (i > 0)) & ((kj < 2 * tq) | (kj >= 3 * tq) | (i < nb - 1))
    bias = jnp.where(ok, 0.0, NEG_INF).astype(F32)
    q = q_ref[...]
    acc = [None] * A_GROUPS
    for t in range(A_KV_HEADS):
        qs = _stack_masked([q[:, j * BLK:(j + 1) * BLK] for j in range(A_GROUPS)], [t] * A_GROUPS)
        outs = _attend(qs, k, v, tq, [bias] * A_GROUPS,
                       [sink_ref[t * A_GROUPS + j] for j in range(A_GROUPS)])
        for j in range(A_GROUPS):
            acc[j] = outs[j] if t == 0 else jnp.where(_slot_mask(t), outs[j], acc[j])
    o = jnp.concatenate(acc, axis=1)
    o_ref[...] = _gate_out_norm(o, g_ref[...], h_ref[...], gate_ref[...], wout_ref,
                                lng_ref[...], lnb_ref[...])


def _attn_a(p, pc, h, mod, layer, wout, sink, lng, lnb, *, tq=128):
    bsz, seq, _ = h.shape
    n_ctx = pc.shape[1]
    nb = seq // tq
    kcol = 2 * D_MODEL // A_KV_DIM
    vcol = kcol + 1
    kv = lambda col, off: pl.BlockSpec(
        (None, tq, A_KV_DIM), lambda b, i: (b, jnp.clip(i + off, 0, nb - 1), col))
    in_specs = [
        pl.BlockSpec(memory_space=pltpu.SMEM),
        pl.BlockSpec((None, tq, D_MODEL), lambda b, i: (b, i, 0)),
        pl.BlockSpec((None, tq, D_MODEL), lambda b, i: (b, i, 1)),
        kv(kcol, -1), kv(kcol, 0), kv(kcol, 1),
        kv(vcol, -1), kv(vcol, 0), kv(vcol, 1),
        pl.BlockSpec((None, n_ctx, A_KV_DIM), lambda b, i: (b, 0, kcol)),
        pl.BlockSpec((None, n_ctx, A_KV_DIM), lambda b, i: (b, 0, vcol)),
        pl.BlockSpec((None, tq, D_MODEL), lambda b, i: (b, i, 0)),
        pl.BlockSpec((None, None, None, 1, D_MODEL), lambda b, i: (layer, 2, b, 0, 0)),
        pl.BlockSpec((D_MODEL, D_MODEL), lambda b, i: (0, 0)),
        pl.BlockSpec((None, 1, D_MODEL), lambda b, i: (layer, 0, 0)),
        pl.BlockSpec((None, 1, D_MODEL), lambda b, i: (layer, 0, 0)),
    ]
    return pl.pallas_call(
        functools.partial(_attn_a_kernel, tq=tq),
        grid=(bsz, nb),
        in_specs=in_specs,
        out_specs=pl.BlockSpec((None, tq, D_MODEL), lambda b, i: (b, i, 0)),
        out_shape=jax.ShapeDtypeStruct((bsz, seq, D_MODEL), F32),
        compiler_params=pltpu.CompilerParams(
            dimension_semantics=("arbitrary", "arbitrary"), vmem_limit_bytes=VMEM_LIMIT),
        name="attn_a",
    )(sink, p, p, p, p, p, p, p, p, pc, pc, h, mod, wout, lng, lnb)


def _attn_b_kernel(q_ref, k_ref, v_ref, g_ref, kx_ref, vx_ref, bias_ref, h_ref, gate_ref,
                   wout_ref, lng_ref, lnb_ref, o_ref, *, rows):
    r = pl.program_id(1)
    band = B_KH * GRID_W
    start = pl.multiple_of(jnp.clip(r - B_KH // 2, 0, rows - B_KH) * GRID_W, GRID_W)
    q = q_ref[...]
    blocks = []
    for blk in range(N_BLKS):
        cols = slice(blk * BLK, (blk + 1) * BLK)
        qs = _stack_masked([q[:, cols]] * SLOTS, list(range(SLOTS)))
        k = jnp.concatenate([k_ref[pl.ds(start, band), cols], kx_ref[:, cols]], axis=0)
        v = jnp.concatenate([v_ref[pl.ds(start, band), cols], vx_ref[:, cols]], axis=0)
        n_ctx = kx_ref.shape[0]
        biases = []
        for s in range(SLOTS):
            row0 = (blk * SLOTS + s) * GRID_W
            biases.append(jnp.concatenate(
                [bias_ref[row0:row0 + GRID_W, :], jnp.zeros((GRID_W, n_ctx), F32)], axis=1))
        outs = _attend(qs, k, v, GRID_W, biases, [None] * SLOTS)
        ob = outs[0]
        for s in range(1, SLOTS):
            ob = jnp.where(_slot_mask(s), outs[s], ob)
        blocks.append(ob)
    o = jnp.concatenate(blocks, axis=1)
    o_ref[...] = _gate_out_norm(o, g_ref[...], h_ref[...], gate_ref[...], wout_ref,
                                lng_ref[...], lnb_ref[...])


def _attn_b(p, pc, h, mod, layer, wout, bias_tab, lng, lnb):
    bsz, seq, _ = h.shape
    n_ctx = pc.shape[1]
    rows = seq // GRID_W
    delta = lambda r: r - jnp.clip(r - B_KH // 2, 0, rows - B_KH)
    in_specs = [
        pl.BlockSpec((None, GRID_W, D_MODEL), lambda b, r: (b, r, 0)),
        pl.BlockSpec((None, seq, D_MODEL), lambda b, r: (b, 0, 1)),
        pl.BlockSpec((None, seq, D_MODEL), lambda b, r: (b, 0, 2)),
        pl.BlockSpec((None, GRID_W, D_MODEL), lambda b, r: (b, r, 3)),
        pl.BlockSpec((None, n_ctx, D_MODEL), lambda b, r: (b, 0, 1)),
        pl.BlockSpec((None, n_ctx, D_MODEL), lambda b, r: (b, 0, 2)),
        pl.BlockSpec((None, N_HEADS * GRID_W, B_KH * GRID_W), lambda b, r: (delta(r), 0, 0)),
        pl.BlockSpec((None, GRID_W, D_MODEL), lambda b, r: (b, r, 0)),
        pl.BlockSpec((None, None, None, 1, D_MODEL), lambda b, r: (layer, 2, b, 0, 0)),
        pl.BlockSpec((D_MODEL, D_MODEL), lambda b, r: (0, 0)),
        pl.BlockSpec((None, 1, D_MODEL), lambda b, r: (layer, 0, 0)),
        pl.BlockSpec((None, 1, D_MODEL), lambda b, r: (layer, 0, 0)),
    ]
    return pl.pallas_call(
        functools.partial(_attn_b_kernel, rows=rows),
        grid=(bsz, rows),
        in_specs=in_specs,
        out_specs=pl.BlockSpec((None, GRID_W, D_MODEL), lambda b, r: (b, r, 0)),
        out_shape=jax.ShapeDtypeStruct((bsz, seq, D_MODEL), F32),
        compiler_params=pltpu.CompilerParams(
            dimension_semantics=("arbitrary", "arbitrary"), vmem_limit_bytes=VMEM_LIMIT),
        name="attn_b",
    )(p, p, p, p, pc, pc, bias_tab, h, mod, wout, lng, lnb)


def _ctx_attn_kernel(*refs, mixer_a):
    if mixer_a:
        sink_ref, q_ref, g_ref, k_ref, v_ref, h_ref, gate_ref, wout_ref, lng_ref, lnb_ref, o_ref = refs
    else:
        q_ref, g_ref, k_ref, v_ref, h_ref, gate_ref, wout_ref, lng_ref, lnb_ref, o_ref = refs
    n_ctx = q_ref.shape[0]
    q = q_ref[...]
    none = [None] * SLOTS
    if mixer_a:
        k, v = k_ref[...], v_ref[...]
        acc = [None] * A_GROUPS
        for t in range(A_KV_HEADS):
            qs = _stack_masked([q[:, j * BLK:(j + 1) * BLK] for j in range(A_GROUPS)], [t] * A_GROUPS)
            outs = _attend(qs, k, v, n_ctx, none,
                           [sink_ref[t * A_GROUPS + j] for j in range(A_GROUPS)])
            for j in range(A_GROUPS):
                acc[j] = outs[j] if t == 0 else jnp.where(_slot_mask(t), outs[j], acc[j])
        o = jnp.concatenate(acc, axis=1)
    else:
        blocks = []
        for blk in range(N_BLKS):
            cols = slice(blk * BLK, (blk + 1) * BLK)
            qs = _stack_masked([q[:, cols]] * SLOTS, list(range(SLOTS)))
            outs = _attend(qs, k_ref[:, cols], v_ref[:, cols], n_ctx, none, none)
            ob = outs[0]
            for s in range(1, SLOTS):
                ob = jnp.where(_slot_mask(s), outs[s], ob)
            blocks.append(ob)
        o = jnp.concatenate(blocks, axis=1)
    o_ref[...] = _gate_out_norm(o, g_ref[...], h_ref[...], gate_ref[...], wout_ref,
                                lng_ref[...], lnb_ref[...])


def _ctx_attn(pc, hc, mod, layer, wout, lng, lnb, *, mod_row, mixer_a, sink=None):
    bsz, n_ctx, _ = hc.shape
    if mixer_a:
        kv_w, qcol, gcol, kcol = A_KV_DIM, 0, 1, 2 * D_MODEL // A_KV_DIM
    else:
        kv_w, qcol, gcol, kcol = D_MODEL, 0, 3, 1
    in_specs = [
        pl.BlockSpec((None, n_ctx, D_MODEL), lambda b: (b, 0, qcol)),
        pl.BlockSpec((None, n_ctx, D_MODEL), lambda b: (b, 0, gcol)),
        pl.BlockSpec((None, n_ctx, kv_w), lambda b: (b, 0, kcol)),
        pl.BlockSpec((None, n_ctx, kv_w), lambda b: (b, 0, kcol + 1)),
        pl.BlockSpec((None, n_ctx, D_MODEL), lambda b: (b, 0, 0)),
        pl.BlockSpec((None, None, None, 1, D_MODEL), lambda b: (layer, 2, mod_row, 0, 0)),
        pl.BlockSpec((D_MODEL, D_MODEL), lambda b: (0, 0)),
        pl.BlockSpec((None, 1, D_MODEL), lambda b: (layer, 0, 0)),
        pl.BlockSpec((None, 1, D_MODEL), lambda b: (layer, 0, 0)),
    ]
    args = [pc, pc, pc, pc, hc, mod, wout, lng, lnb]
    if mixer_a:
        in_specs = [pl.BlockSpec(memory_space=pltpu.SMEM)] + in_specs
        args = [sink] + args
    return pl.pallas_call(
        functools.partial(_ctx_attn_kernel, mixer_a=mixer_a),
        grid=(bsz,),
        in_specs=in_specs,
        out_specs=pl.BlockSpec((None, n_ctx, D_MODEL), lambda b: (b, 0, 0)),
        out_shape=jax.ShapeDtypeStruct((bsz, n_ctx, D_MODEL), F32),
        compiler_params=pltpu.CompilerParams(
            dimension_semantics=("arbitrary",), vmem_limit_bytes=VMEM_LIMIT),
        name="ctx_attn",
    )(*args)


def _head_perm():
    j, t, d = np.meshgrid(np.arange(A_GROUPS), np.arange(A_KV_HEADS), np.arange(HEAD_DIM), indexing="ij")
    return ((t * A_GROUPS + j) * HEAD_DIM + d).reshape(-1)


def _prep_a(w_in, w_out):
    perm = _head_perm()
    q = w_in[:, :D_MODEL][:, perm] * Q_SCALE
    k = w_in[:, D_MODEL:D_MODEL + A_KV_DIM]
    v = w_in[:, D_MODEL + A_KV_DIM:D_MODEL + 2 * A_KV_DIM]
    g = w_in[:, D_MODEL + 2 * A_KV_DIM:][:, perm]
    return jnp.concatenate([q, g, k, v], axis=1).astype(BF16), w_out[perm, :].astype(BF16)


def _prep_b(w_in, w_out):
    w = jnp.concatenate([w_in[:, :D_MODEL] * Q_SCALE, w_in[:, D_MODEL:]], axis=1)
    return w.astype(BF16), w_out.astype(BF16)


def _rope_tables(seq):
    t = jnp.arange(seq)
    row = (t // GRID_W).astype(F32)
    col = (t % GRID_W).astype(F32)
    n_freq = HEAD_DIM // 4
    inv = ROPE_BASE ** (-jnp.arange(n_freq, dtype=F32) / n_freq)
    ang_r = row[:, None] * inv[None]
    ang_c = col[:, None] * inv[None]
    zero = jnp.zeros_like(ang_r)
    cos = jnp.concatenate([jnp.cos(ang_r)] * 2 + [jnp.cos(ang_c)] * 2, axis=1)
    sina = jnp.concatenate([-jnp.sin(ang_r), zero, -jnp.sin(ang_c), zero], axis=1)
    sinb = jnp.concatenate([zero, jnp.sin(ang_r), zero, jnp.sin(ang_c)], axis=1)
    rep = LANES // HEAD_DIM
    return tuple(jnp.tile(x, (1, rep)) for x in (cos, sina, sinb))


def _b_bias_table(rel_bias):
    qc = np.arange(GRID_W)
    kc = np.arange(GRID_W)
    cs = np.clip(qc - B_KW // 2, 0, GRID_W - B_KW)
    valid = (kc[None, :] >= cs[:, None]) & (kc[None, :] < cs[:, None] + B_KW)
    cidx = np.clip(kc[None, :] - qc[:, None] + (B_KW - 1), 0, 2 * B_KW - 2)
    ridx = np.arange(B_KH)[None, :] - np.arange(B_KH)[:, None] + (B_KH - 1)
    t = rel_bias.astype(F32)[:, ridx[:, None, :, None], cidx[None, :, None, :]]
    t = jnp.where(jnp.asarray(valid)[None, None, :, None, :], t, NEG_INF)
    t = jnp.transpose(t, (1, 0, 2, 3, 4))
    return t.reshape(B_KH, N_HEADS * GRID_W, B_KH * GRID_W)


def kernel(x, c, ctx, c_ctx, w_ada, b_ada, ln_g, ln_b, a_w_in, a_w_out, a_sink, b_w_in, b_w_out, b_rel_bias):
    bsz, seq, _ = x.shape
    assert bsz + 1 <= MOD_ROWS
    cc = jnp.concatenate([c, c_ctx[None, :], jnp.zeros((MOD_ROWS - bsz - 1, D_MODEL), F32)], axis=0)
    mod = _ada(cc, w_ada, b_ada)
    rope = _rope_tables(seq)
    lng = ln_g.reshape(DEPTH, 1, D_MODEL)
    lnb = ln_b.reshape(DEPTH, 1, D_MODEL)
    rope_chunks = tuple(range(D_MODEL // BLK)) + (2 * D_MODEL // BLK,)
    h, hc = x, ctx
    for i in range(DEPTH):
        j = i // 2
        ctx_out = i < DEPTH - 1
        if i % 2 == 0:
            w_in, w_out = _prep_a(a_w_in[j], a_w_out[j])
            p = _inproj(h, mod, i, w_in, tm=512, mod_row=None, rope_tabs=rope, rope_chunks=rope_chunks)
            pc = _inproj(hc, mod, i, w_in, tm=hc.shape[1], mod_row=bsz)
            h = _attn_a(p, pc, h, mod, i, w_out, a_sink[j], lng, lnb)
            if ctx_out:
                hc = _ctx_attn(pc, hc, mod, i, w_out, lng, lnb, mod_row=bsz, mixer_a=True, sink=a_sink[j])
        else:
            w_in, w_out = _prep_b(b_w_in[j], b_w_out[j])
            p = _inproj(h, mod, i, w_in, tm=512, mod_row=None)
            pc = _inproj(hc, mod, i, w_in, tm=hc.shape[1], mod_row=bsz)
            h = _attn_b(p, pc, h, mod, i, w_out, _b_bias_table(b_rel_bias[j]), lng, lnb)
            if ctx_out:
                hc = _ctx_attn(pc, hc, mod, i, w_out, lng, lnb, mod_row=bsz, mixer_a=False)
    return h
```

```python
import functools

import numpy as np
import jax
import jax.numpy as jnp
from jax import lax
from jax.experimental import pallas as pl
from jax.experimental.pallas import tpu as pltpu

D_MODEL = 1024
DEPTH = 4
GRID_W = 64
HEAD_DIM = 64
N_HEADS = 16
A_KV_HEADS = 4
A_GROUPS = 4
A_WINDOW = 128
A_KV_DIM = A_KV_HEADS * HEAD_DIM
B_KH = 8
B_KW = 16
ROPE_BASE = 10000.0
LN_EPS = 1e-5
NEG_INF = -1e30
DEEPNORM_ALPHA = (2.0 * DEPTH) ** 0.25
Q_SCALE = HEAD_DIM ** -0.5

LANES = 128
BLK = 256
SLOTS = BLK // HEAD_DIM
N_BLKS = D_MODEL // BLK
MOD_ROWS = 40
VMEM_LIMIT = 56 * 1024 * 1024

F32 = jnp.float32
BF16 = jnp.bfloat16


def _silu(x):
    return x / (1.0 + jnp.exp(-x))


def _slot_mask(slot):
    lane = lax.broadcasted_iota(jnp.int32, (1, BLK), 1)
    return (lane // HEAD_DIM) == slot


def _ada_kernel(c_ref, w_ref, b_ref, o_ref):
    sc = _silu(c_ref[...])
    o_ref[...] = jnp.dot(sc, w_ref[...], preferred_element_type=F32,
                         precision=lax.Precision.HIGHEST) + b_ref[...]


def _ada(cc, w_ada, b_ada):
    out = pl.pallas_call(
        _ada_kernel,
        grid=(DEPTH, 3),
        in_specs=[
            pl.BlockSpec((MOD_ROWS, D_MODEL), lambda d, j: (0, 0)),
            pl.BlockSpec((None, D_MODEL, D_MODEL), lambda d, j: (d, 0, j)),
            pl.BlockSpec((None, None, 1, D_MODEL), lambda d, j: (d, j, 0, 0)),
        ],
        out_specs=pl.BlockSpec((None, None, MOD_ROWS, D_MODEL), lambda d, j: (d, j, 0, 0)),
        out_shape=jax.ShapeDtypeStruct((DEPTH, 3, MOD_ROWS, D_MODEL), F32),
        compiler_params=pltpu.CompilerParams(
            dimension_semantics=("arbitrary", "arbitrary"), vmem_limit_bytes=VMEM_LIMIT),
        name="ada_mod",
    )(cc, w_ada, b_ada.reshape(DEPTH, 3, 1, D_MODEL))
    return out.reshape(DEPTH, 3, MOD_ROWS, 1, D_MODEL)


def _inproj_kernel(*refs, n_out, rope_chunks, chunk):
    if rope_chunks:
        x_ref, shift_ref, scale_ref, w_ref, cos_ref, sina_ref, sinb_ref, o_ref = refs
    else:
        x_ref, shift_ref, scale_ref, w_ref, o_ref = refs
    u = (x_ref[...] * (1.0 + scale_ref[...]) + shift_ref[...]).astype(BF16)
    for ci in range(n_out // chunk):
        c0 = ci * chunk
        acc = jnp.dot(u, w_ref[:, c0:c0 + chunk], preferred_element_type=F32)
        if ci in rope_chunks:
            cos, sina, sinb = cos_ref[...], sina_ref[...], sinb_ref[...]
            parts = []
            for l0 in range(0, chunk, LANES):
                a = acc[:, l0:l0 + LANES]
                parts.append(a * cos + pltpu.roll(a, LANES - 16, 1) * sina + pltpu.roll(a, 16, 1) * sinb)
            acc = jnp.concatenate(parts, axis=1)
        o_ref[:, c0:c0 + chunk] = acc.astype(BF16)


def _inproj(h, mod, layer, w, *, tm, mod_row, rope_tabs=None, rope_chunks=()):
    bsz, seq, _ = h.shape
    n_out = w.shape[1]
    chunk = BLK
    if mod_row is None:
        row = lambda b: b
    else:
        row = lambda b: mod_row
    in_specs = [
        pl.BlockSpec((None, tm, D_MODEL), lambda b, s: (b, s, 0)),
        pl.BlockSpec((None, None, None, 1, D_MODEL), lambda b, s: (layer, 0, row(b), 0, 0)),
        pl.BlockSpec((None, None, None, 1, D_MODEL), lambda b, s: (layer, 1, row(b), 0, 0)),
        pl.BlockSpec((D_MODEL, n_out), lambda b, s: (0, 0)),
    ]
    args = [h, mod, mod, w]
    if rope_chunks:
        in_specs += [pl.BlockSpec((tm, LANES), lambda b, s: (s, 0))] * 3
        args += list(rope_tabs)
    return pl.pallas_call(
        functools.partial(_inproj_kernel, n_out=n_out, rope_chunks=tuple(rope_chunks), chunk=chunk),
        grid=(bsz, seq // tm),
        in_specs=in_specs,
        out_specs=pl.BlockSpec((None, tm, n_out), lambda b, s: (b, s, 0)),
        out_shape=jax.ShapeDtypeStruct((bsz, seq, n_out), BF16),
        compiler_params=pltpu.CompilerParams(
            dimension_semantics=("arbitrary", "arbitrary"), vmem_limit_bytes=VMEM_LIMIT),
        name="inproj",
    )(*args)


def _stack_masked(q_blocks, slots):
    return jnp.concatenate(
        [jnp.where(_slot_mask(s), q, jnp.zeros_like(q)) for q, s in zip(q_blocks, slots)], axis=0)

def _attend(qs, k, v, tq, biases, sinks):
    s = lax.dot_general(qs, k, (((1,), (1,)), ((), ())), preferred_element_type=F32)
    ps, invs = [], []
    for a in range(SLOTS):
        sa = s[a * tq:(a + 1) * tq]
        if biases[a] is not None:
            sa = sa + biases[a]
        m = jnp.max(sa, axis=-1, keepdims=True)
        if sinks[a] is not None:
            m = jnp.maximum(m, sinks[a])
        p = jnp.exp(sa - m)
        l = jnp.sum(p, axis=-1, keepdims=True)
        if sinks[a] is not None:
            l = l + jnp.exp(sinks[a] - m)
        ps.append(p.astype(BF16))
        invs.append(1.0 / l)
    o = jnp.dot(jnp.concatenate(ps, axis=0), v, preferred_element_type=F32)
    return [o[a * tq:(a + 1) * tq] * invs[a] for a in range(SLOTS)]


def _gate_out_norm(o, g, h, gate, wout_ref, lng, lnb):
    gated = (o * _silu(g.astype(F32))).astype(BF16)
    y = jnp.dot(gated, wout_ref[...], preferred_element_type=F32)
    r = DEEPNORM_ALPHA * h + gate * y
    mu = jnp.mean(r, axis=-1, keepdims=True)
    rc = r - mu
    var = jnp.mean(rc * rc, axis=-1, keepdims=True)
    return rc * lax.rsqrt(var + LN_EPS) * lng + lnb


def _attn_a_kernel(sink_ref, q_ref, g_ref, kp_ref, kc_ref, kn_ref, vp_ref, vc_ref, vn_ref,
                   kx_ref, vx_ref, h_ref, gate_ref, wout_ref, lng_ref, lnb_ref, o_ref, *, tq):
    i = pl.program_id(1)
    nb = pl.num_programs(1)
    n_ctx = kx_ref.shape[0]
    k = jnp.concatenate([kp_ref[...], kc_ref[...], kn_ref[...], kx_ref[...]], axis=0)
    v = jnp.concatenate([vp_ref[...], vc_ref[...], vn_ref[...], vx_ref[...]], axis=0)
    nk = 3 * tq + n_ctx
    qi = lax.broadcasted_iota(jnp.int32, (tq, nk), 0)
    kj = lax.broadcasted_iota(jnp.int32, (tq, nk), 1)
    rel = kj - tq - qi
    ok = (jnp.abs(rel) <= A_WINDOW) | (kj >= 3 * tq)
    ok = ok & ((kj >= tq) | (i > 0)) & ((kj < 2 * tq) | (kj >= 3 * tq) | (i < nb - 1))
    bias = jnp.where(ok, 0.0, NEG_INF).astype(F32)
    q = q_ref[...]
    acc = [None] * A_GROUPS
    for t in range(A_KV_HEADS):
        qs = _stack_masked([q[:, j * BLK:(j + 1) * BLK] for j in range(A_GROUPS)], [t] * A_GROUPS)
        outs = _attend(qs, k, v, tq, [bias] * A_GROUPS,
                       [sink_ref[t * A_GROUPS + j] for j in range(A_GROUPS)])
        for j in range(A_GROUPS):
            acc[j] = outs[j] if t == 0 else jnp.where(_slot_mask(t), outs[j], acc[j])
    o = jnp.concatenate(acc, axis=1)
    o_ref[...] = _gate_out_norm(o, g_ref[...], h_ref[...], gate_ref[...], wout_ref,
                                lng_ref[...], lnb_ref[...])


def _attn_a(p, pc, h, mod, layer, wout, sink, lng, lnb, *, tq=128):
    bsz, seq, _ = h.shape
    n_ctx = pc.shape[1]
    nb = seq // tq
    kcol = 2 * D_MODEL // A_KV_DIM
    vcol = kcol + 1
    kv = lambda col, off: pl.BlockSpec(
        (None, tq, A_KV_DIM), lambda b, i: (b, jnp.clip(i + off, 0, nb - 1), col))
    in_specs = [
        pl.BlockSpec(memory_space=pltpu.SMEM),
        pl.BlockSpec((None, tq, D_MODEL), lambda b, i: (b, i, 0)),
        pl.BlockSpec((None, tq, D_MODEL), lambda b, i: (b, i, 1)),
        kv(kcol, -1), kv(kcol, 0), kv(kcol, 1),
        kv(vcol, -1), kv(vcol, 0), kv(vcol, 1),
        pl.BlockSpec((None, n_ctx, A_KV_DIM), lambda b, i: (b, 0, kcol)),
        pl.BlockSpec((None, n_ctx, A_KV_DIM), lambda b, i: (b, 0, vcol)),
        pl.BlockSpec((None, tq, D_MODEL), lambda b, i: (b, i, 0)),
        pl.BlockSpec((None, None, None, 1, D_MODEL), lambda b, i: (layer, 2, b, 0, 0)),
        pl.BlockSpec((D_MODEL, D_MODEL), lambda b, i: (0, 0)),
        pl.BlockSpec((None, 1, D_MODEL), lambda b, i: (layer, 0, 0)),
        pl.BlockSpec((None, 1, D_MODEL), lambda b, i: (layer, 0, 0)),
    ]
    return pl.pallas_call(
        functools.partial(_attn_a_kernel, tq=tq),
        grid=(bsz, nb),
        in_specs=in_specs,
        out_specs=pl.BlockSpec((None, tq, D_MODEL), lambda b, i: (b, i, 0)),
        out_shape=jax.ShapeDtypeStruct((bsz, seq, D_MODEL), F32),
        compiler_params=pltpu.CompilerParams(
            dimension_semantics=("arbitrary", "arbitrary"), vmem_limit_bytes=VMEM_LIMIT),
        name="attn_a",
    )(sink, p, p, p, p, p, p, p, p, pc, pc, h, mod, wout, lng, lnb)


def _attn_b_kernel(q_ref, k_ref, v_ref, g_ref, kx_ref, vx_ref, bias_ref, h_ref, gate_ref,
                   wout_ref, lng_ref, lnb_ref, o_ref, *, rows):
    r = pl.program_id(1)
    band = B_KH * GRID_W
    start = pl.multiple_of(jnp.clip(r - B_KH // 2, 0, rows - B_KH) * GRID_W, GRID_W)
    q = q_ref[...]
    blocks = []
    for blk in range(N_BLKS):
        cols = slice(blk * BLK, (blk + 1) * BLK)
        qs = _stack_masked([q[:, cols]] * SLOTS, list(range(SLOTS)))
        k = jnp.concatenate([k_ref[pl.ds(start, band), cols], kx_ref[:, cols]], axis=0)
        v = jnp.concatenate([v_ref[pl.ds(start, band), cols], vx_ref[:, cols]], axis=0)
        n_ctx = kx_ref.shape[0]
        biases = []
        for s in range(SLOTS):
            row0 = (blk * SLOTS + s) * GRID_W
            biases.append(jnp.concatenate(
                [bias_ref[row0:row0 + GRID_W, :], jnp.zeros((GRID_W, n_ctx), F32)], axis=1))
        outs = _attend(qs, k, v, GRID_W, biases, [None] * SLOTS)
        ob = outs[0]
        for s in range(1, SLOTS):
            ob = jnp.where(_slot_mask(s), outs[s], ob)
        blocks.append(ob)
    o = jnp.concatenate(blocks, axis=1)
    o_ref[...] = _gate_out_norm(o, g_ref[...], h_ref[...], gate_ref[...], wout_ref,
                                lng_ref[...], lnb_ref[...])


def _attn_b(p, pc, h, mod, layer, wout, bias_tab, lng, lnb):
    bsz, seq, _ = h.shape
    n_ctx = pc.shape[1]
    rows = seq // GRID_W
    delta = lambda r: r - jnp.clip(r - B_KH // 2, 0, rows - B_KH)
    in_specs = [
        pl.BlockSpec((None, GRID_W, D_MODEL), lambda b, r: (b, r, 0)),
        pl.BlockSpec((None, seq, D_MODEL), lambda b, r: (b, 0, 1)),
        pl.BlockSpec((None, seq, D_MODEL), lambda b, r: (b, 0, 2)),
        pl.BlockSpec((None, GRID_W, D_MODEL), lambda b, r: (b, r, 3)),
        pl.BlockSpec((None, n_ctx, D_MODEL), lambda b, r: (b, 0, 1)),
        pl.BlockSpec((None, n_ctx, D_MODEL), lambda b, r: (b, 0, 2)),
        pl.BlockSpec((None, N_HEADS * GRID_W, B_KH * GRID_W), lambda b, r: (delta(r), 0, 0)),
        pl.BlockSpec((None, GRID_W, D_MODEL), lambda b, r: (b, r, 0)),
        pl.BlockSpec((None, None, None, 1, D_MODEL), lambda b, r: (layer, 2, b, 0, 0)),
        pl.BlockSpec((D_MODEL, D_MODEL), lambda b, r: (0, 0)),
        pl.BlockSpec((None, 1, D_MODEL), lambda b, r: (layer, 0, 0)),
        pl.BlockSpec((None, 1, D_MODEL), lambda b, r: (layer, 0, 0)),
    ]
    return pl.pallas_call(
        functools.partial(_attn_b_kernel, rows=rows),
        grid=(bsz, rows),
        in_specs=in_specs,
        out_specs=pl.BlockSpec((None, GRID_W, D_MODEL), lambda b, r: (b, r, 0)),
        out_shape=jax.ShapeDtypeStruct((bsz, seq, D_MODEL), F32),
        compiler_params=pltpu.CompilerParams(
            dimension_semantics=("arbitrary", "arbitrary"), vmem_limit_bytes=VMEM_LIMIT),
        name="attn_b",
    )(p, p, p, p, pc, pc, bias_tab, h, mod, wout, lng, lnb)


def _ctx_attn_kernel(*refs, mixer_a):
    if mixer_a:
        sink_ref, q_ref, g_ref, k_ref, v_ref, h_ref, gate_ref, wout_ref, lng_ref, lnb_ref, o_ref = refs
    else:
        q_ref, g_ref, k_ref, v_ref, h_ref, gate_ref, wout_ref, lng_ref, lnb_ref, o_ref = refs
    n_ctx = q_ref.shape[0]
    q = q_ref[...]
    none = [None] * SLOTS
    if mixer_a:
        k, v = k_ref[...], v_ref[...]
        acc = [None] * A_GROUPS
        for t in range(A_KV_HEADS):
            qs = _stack_masked([q[:, j * BLK:(j + 1) * BLK] for j in range(A_GROUPS)], [t] * A_GROUPS)
            outs = _attend(qs, k, v, n_ctx, none,
                           [sink_ref[t * A_GROUPS + j] for j in range(A_GROUPS)])
            for j in range(A_GROUPS):
                acc[j] = outs[j] if t == 0 else jnp.where(_slot_mask(t), outs[j], acc[j])
        o = jnp.concatenate(acc, axis=1)
    else:
        blocks = []
        for blk in range(N_BLKS):
            cols = slice(blk * BLK, (blk + 1) * BLK)
            qs = _stack_masked([q[:, cols]] * SLOTS, list(range(SLOTS)))
            outs = _attend(qs, k_ref[:, cols], v_ref[:, cols], n_ctx, none, none)
            ob = outs[0]
            for s in range(1, SLOTS):
                ob = jnp.where(_slot_mask(s), outs[s], ob)
            blocks.append(ob)
        o = jnp.concatenate(blocks, axis=1)
    o_ref[...] = _gate_out_norm(o, g_ref[...], h_ref[...], gate_ref[...], wout_ref,
                                lng_ref[...], lnb_ref[...])


def _ctx_attn(pc, hc, mod, layer, wout, lng, lnb, *, mod_row, mixer_a, sink=None):
    bsz, n_ctx, _ = hc.shape
    if mixer_a:
        kv_w, qcol, gcol, kcol = A_KV_DIM, 0, 1, 2 * D_MODEL // A_KV_DIM
    else:
        kv_w, qcol, gcol, kcol = D_MODEL, 0, 3, 1
    in_specs = [
        pl.BlockSpec((None, n_ctx, D_MODEL), lambda b: (b, 0, qcol)),
        pl.BlockSpec((None, n_ctx, D_MODEL), lambda b: (b, 0, gcol)),
        pl.BlockSpec((None, n_ctx, kv_w), lambda b: (b, 0, kcol)),
        pl.BlockSpec((None, n_ctx, kv_w), lambda b: (b, 0, kcol + 1)),
        pl.BlockSpec((None, n_ctx, D_MODEL), lambda b: (b, 0, 0)),
        pl.BlockSpec((None, None, None, 1, D_MODEL), lambda b: (layer, 2, mod_row, 0, 0)),
        pl.BlockSpec((D_MODEL, D_MODEL), lambda b: (0, 0)),
        pl.BlockSpec((None, 1, D_MODEL), lambda b: (layer, 0, 0)),
        pl.BlockSpec((None, 1, D_MODEL), lambda b: (layer, 0, 0)),
    ]
    args = [pc, pc, pc, pc, hc, mod, wout, lng, lnb]
    if mixer_a:
        in_specs = [pl.BlockSpec(memory_space=pltpu.SMEM)] + in_specs
        args = [sink] + args
    return pl.pallas_call(
        functools.partial(_ctx_attn_kernel, mixer_a=mixer_a),
        grid=(bsz,),
        in_specs=in_specs,
        out_specs=pl.BlockSpec((None, n_ctx, D_MODEL), lambda b: (b, 0, 0)),
        out_shape=jax.ShapeDtypeStruct((bsz, n_ctx, D_MODEL), F32),
        compiler_params=pltpu.CompilerParams(
            dimension_semantics=("arbitrary",), vmem_limit_bytes=VMEM_LIMIT),
        name="ctx_attn",
    )(*args)


def _head_perm():
    j, t, d = np.meshgrid(np.arange(A_GROUPS), np.arange(A_KV_HEADS), np.arange(HEAD_DIM), indexing="ij")
    return ((t * A_GROUPS + j) * HEAD_DIM + d).reshape(-1)


def _prep_a(w_in, w_out):
    perm = _head_perm()
    q = w_in[:, :D_MODEL][:, perm] * Q_SCALE
    k = w_in[:, D_MODEL:D_MODEL + A_KV_DIM]
    v = w_in[:, D_MODEL + A_KV_DIM:D_MODEL + 2 * A_KV_DIM]
    g = w_in[:, D_MODEL + 2 * A_KV_DIM:][:, perm]
    return jnp.concatenate([q, g, k, v], axis=1).astype(BF16), w_out[perm, :].astype(BF16)


def _prep_b(w_in, w_out):
    w = jnp.concatenate([w_in[:, :D_MODEL] * Q_SCALE, w_in[:, D_MODEL:]], axis=1)
    return w.astype(BF16), w_out.astype(BF16)


def _rope_tables(seq):
    t = jnp.arange(seq)
    row = (t // GRID_W).astype(F32)
    col = (t % GRID_W).astype(F32)
    n_freq = HEAD_DIM // 4
    inv = ROPE_BASE ** (-jnp.arange(n_freq, dtype=F32) / n_freq)
    ang_r = row[:, None] * inv[None]
    ang_c = col[:, None] * inv[None]
    zero = jnp.zeros_like(ang_r)
    cos = jnp.concatenate([jnp.cos(ang_r)] * 2 + [jnp.cos(ang_c)] * 2, axis=1)
    sina = jnp.concatenate([-jnp.sin(ang_r), zero, -jnp.sin(ang_c), zero], axis=1)
    sinb = jnp.concatenate([zero, jnp.sin(ang_r), zero, jnp.sin(ang_c)], axis=1)
    rep = LANES // HEAD_DIM
    return tuple(jnp.tile(x, (1, rep)) for x in (cos, sina, sinb))


def _bias_tab_kernel(rb_ref, o_ref):
    shape = (GRID_W, LANES)
    qc = lax.broadcasted_iota(jnp.int32, shape, 0)
    lane = lax.broadcasted_iota(jnp.int32, shape, 1)
    kc = lane % GRID_W
    cs = jnp.clip(qc - B_KW // 2, 0, GRID_W - B_KW)
    valid = (kc >= cs) & (kc < cs + B_KW)
    left = lane < GRID_W
    sh_left = LANES - (B_KW - 1)
    sh_right = (sh_left + GRID_W) % LANES
    rolled = []
    for dr in range(2 * B_KH - 1):
        row = jnp.broadcast_to(rb_ref[dr:dr + 1, :], shape)
        rolled.append((pltpu.roll(row, sh_left, 1, stride=1, stride_axis=0),
                       pltpu.roll(row, sh_right, 1, stride=1, stride_axis=0)))
    for delta in range(B_KH):
        for pair in range(B_KH // 2):
            dr = 2 * pair - delta + B_KH - 1
            tile = jnp.where(left, rolled[dr][0], rolled[dr + 1][1])
            o_ref[delta, :, pair * LANES:(pair + 1) * LANES] = jnp.where(valid, tile, NEG_INF)


def _b_bias_table(rel_bias):
    n_dr, n_dc = rel_bias.shape[1], rel_bias.shape[2]
    rb = jnp.pad(rel_bias.astype(F32), ((0, 0), (0, 0), (0, LANES - n_dc)))
    return pl.pallas_call(
        _bias_tab_kernel,
        grid=(N_HEADS,),
        in_specs=[pl.BlockSpec((None, n_dr, LANES), lambda hd: (hd, 0, 0))],
        out_specs=pl.BlockSpec((B_KH, GRID_W, B_KH * GRID_W), lambda hd: (0, hd, 0)),
        out_shape=jax.ShapeDtypeStruct((B_KH, N_HEADS * GRID_W, B_KH * GRID_W), F32),
        compiler_params=pltpu.CompilerParams(
            dimension_semantics=("arbitrary",), vmem_limit_bytes=VMEM_LIMIT),
        name="bias_tab",
    )(rb)


def kernel(x, c, ctx, c_ctx, w_ada, b_ada, ln_g, ln_b, a_w_in, a_w_out, a_sink, b_w_in, b_w_out, b_rel_bias):
    bsz, seq, _ = x.shape
    assert bsz + 1 <= MOD_ROWS
    cc = jnp.concatenate([c, c_ctx[None, :], jnp.zeros((MOD_ROWS - bsz - 1, D_MODEL), F32)], axis=0)
    mod = _ada(cc, w_ada, b_ada)
    rope = _rope_tables(seq)
    lng = ln_g.reshape(DEPTH, 1, D_MODEL)
    lnb = ln_b.reshape(DEPTH, 1, D_MODEL)
    rope_chunks = tuple(range(D_MODEL // BLK)) + (2 * D_MODEL // BLK,)
    h, hc = x, ctx
    for i in range(DEPTH):
        j = i // 2
        ctx_out = i < DEPTH - 1
        if i % 2 == 0:
            w_in, w_out = _prep_a(a_w_in[j], a_w_out[j])
            p = _inproj(h, mod, i, w_in, tm=512, mod_row=None, rope_tabs=rope, rope_chunks=rope_chunks)
            pc = _inproj(hc, mod, i, w_in, tm=hc.shape[1], mod_row=bsz)
            h = _attn_a(p, pc, h, mod, i, w_out, a_sink[j], lng, lnb)
            if ctx_out:
                hc = _ctx_attn(pc, hc, mod, i, w_out, lng, lnb, mod_row=bsz, mixer_a=True, sink=a_sink[j])
        else:
            w_in, w_out = _prep_b(b_w_in[j], b_w_out[j])
            p = _inproj(h, mod, i, w_in, tm=512, mod_row=None)
            pc = _inproj(hc, mod, i, w_in, tm=hc.shape[1], mod_row=bsz)
            h = _attn_b(p, pc, h, mod, i, w_out, _b_bias_table(b_rel_bias[j]), lng, lnb)
            if ctx_out:
                hc = _ctx_attn(pc, hc, mod, i, w_out, lng, lnb, mod_row=bsz, mixer_a=False)
    return h
```

```python
import functools
import math

import numpy as np
import jax
import jax.numpy as jnp
from jax import lax
from jax.experimental import pallas as pl
from jax.experimental.pallas import tpu as pltpu

D_MODEL = 1024
DEPTH = 4
GRID_W = 64
HEAD_DIM = 64
N_HEADS = 16
A_KV_HEADS = 4
A_GROUPS = 4
A_WINDOW = 128
A_KV_DIM = A_KV_HEADS * HEAD_DIM
B_KH = 8
B_KW = 16
ROPE_BASE = 10000.0
LN_EPS = 1e-5
NEG_INF = -1e30
DEEPNORM_ALPHA = (2.0 * DEPTH) ** 0.25
Q_SCALE = HEAD_DIM ** -0.5
LOG2E = math.log2(math.e)

LANES = 128
BLK = 256
SLOTS = BLK // HEAD_DIM
N_BLKS = D_MODEL // BLK
MOD_ROWS = 40
VMEM_LIMIT = 56 * 1024 * 1024
A_TQ = 128
A_SUBS = 4
ONES_ROWS = 16
B_ROWS = 4
N_DR = 2 * B_KH - 1
N_PAIR = B_KH // 2

assert A_TQ == A_WINDOW

F32 = jnp.float32
BF16 = jnp.bfloat16
NT = (((1,), (1,)), ((), ()))


def _silu(x):
    return x / (1.0 + jnp.exp(-x))


def _slot_mask(slot):
    lane = lax.broadcasted_iota(jnp.int32, (1, BLK), 1)
    return (lane // HEAD_DIM) == slot


def _params(n_axes):
    return pltpu.CompilerParams(
        dimension_semantics=("arbitrary",) * n_axes, vmem_limit_bytes=VMEM_LIMIT)


def _ada_kernel(c_ref, w_ref, b_ref, o_ref):
    sc = _silu(c_ref[...])
    o_ref[...] = jnp.dot(sc, w_ref[...], preferred_element_type=F32,
                         precision=lax.Precision.HIGHEST) + b_ref[...]


def _ada(cc, w_ada, b_ada):
    out = pl.pallas_call(
        _ada_kernel,
        grid=(DEPTH, 3),
        in_specs=[
            pl.BlockSpec((MOD_ROWS, D_MODEL), lambda d, j: (0, 0)),
            pl.BlockSpec((None, D_MODEL, D_MODEL), lambda d, j: (d, 0, j)),
            pl.BlockSpec((None, None, 1, D_MODEL), lambda d, j: (d, j, 0, 0)),
        ],
        out_specs=pl.BlockSpec((None, None, MOD_ROWS, D_MODEL), lambda d, j: (d, j, 0, 0)),
        out_shape=jax.ShapeDtypeStruct((DEPTH, 3, MOD_ROWS, D_MODEL), F32),
        compiler_params=_params(2),
        name="ada_mod",
    )(cc, w_ada, b_ada.reshape(DEPTH, 3, 1, D_MODEL))
    return out.reshape(DEPTH, 3, MOD_ROWS, 1, D_MODEL)


def _inproj_kernel(*refs, n_out, q_chunks, rope_chunks, with_vt):
    refs = list(refs)
    x_ref, shift_ref, scale_ref, w_ref = refs[:4]
    refs = refs[4:]
    if with_vt:
        wvt_ref = refs.pop(0)
    if rope_chunks:
        cos_ref, sina_ref, sinb_ref = refs[:3]
        refs = refs[3:]
    o_ref = refs[0]
    u = (x_ref[...] * (1.0 + scale_ref[...]) + shift_ref[...]).astype(BF16)
    for ci in range(n_out // BLK):
        c0 = ci * BLK
        acc = jnp.dot(u, w_ref[:, c0:c0 + BLK], preferred_element_type=F32)
        if ci < q_chunks:
            acc = acc * LOG2E
        if ci in rope_chunks:
            cos, sina, sinb = cos_ref[...], sina_ref[...], sinb_ref[...]
            parts = []
            for l0 in range(0, BLK, LANES):
                a = acc[:, l0:l0 + LANES]
                parts.append(a * cos + pltpu.roll(a, LANES - 16, 1) * sina + pltpu.roll(a, 16, 1) * sinb)
            acc = jnp.concatenate(parts, axis=1)
        o_ref[:, c0:c0 + BLK] = acc.astype(BF16)
    if with_vt:
        vt_ref = refs[1]
        vt_ref[...] = lax.dot_general(wvt_ref[...], u, NT, preferred_element_type=F32).astype(BF16)


def _inproj(h, mod, layer, w, *, tm, mod_row, w_vt=None, rope_tabs=None, rope_chunks=()):
    bsz, seq, _ = h.shape
    n_out = w.shape[1]
    row = (lambda b: b) if mod_row is None else (lambda b: mod_row)
    in_specs = [
        pl.BlockSpec((None, tm, D_MODEL), lambda b, s: (b, s, 0)),
        pl.BlockSpec((None, None, None, 1, D_MODEL), lambda b, s: (layer, 0, row(b), 0, 0)),
        pl.BlockSpec((None, None, None, 1, D_MODEL), lambda b, s: (layer, 1, row(b), 0, 0)),
        pl.BlockSpec((D_MODEL, n_out), lambda b, s: (0, 0)),
    ]
    args = [h, mod, mod, w]
    out_specs = [pl.BlockSpec((None, tm, n_out), lambda b, s: (b, s, 0))]
    out_shape = [jax.ShapeDtypeStruct((bsz, seq, n_out), BF16)]
    if w_vt is not None:
        in_specs.append(pl.BlockSpec(w_vt.shape, lambda b, s: (0, 0)))
        args.append(w_vt)
        out_specs.append(pl.BlockSpec((None, w_vt.shape[0], tm), lambda b, s: (b, 0, s)))
        out_shape.append(jax.ShapeDtypeStruct((bsz, w_vt.shape[0], seq), BF16))
    if rope_chunks:
        in_specs += [pl.BlockSpec((tm, LANES), lambda b, s: (s, 0))] * 3
        args += list(rope_tabs)
    return pl.pallas_call(
        functools.partial(_inproj_kernel, n_out=n_out, q_chunks=N_BLKS,
                          rope_chunks=tuple(rope_chunks), with_vt=w_vt is not None),
        grid=(bsz, seq // tm),
        in_specs=in_specs,
        out_specs=out_specs,
        out_shape=out_shape,
        compiler_params=_params(2),
        name="inproj",
    )(*args)


def _stack_masked(q_blocks, slots):
    return jnp.concatenate(
        [jnp.where(_slot_mask(s), q, jnp.zeros_like(q)) for q, s in zip(q_blocks, slots)], axis=0)


def _b_scores(qs, k):
    return lax.dot_general(qs, k, NT, preferred_element_type=F32)


def _b_softmax(s, tq, biases):
    ps, invs = [], []
    for a in range(SLOTS):
        sa = s[a * tq:(a + 1) * tq]
        if biases[a] is not None:
            nb = biases[a].shape[1]
            sa = jnp.concatenate([sa[:, :nb] + biases[a], sa[:, nb:]], axis=1)
        m = jnp.max(sa, axis=-1, keepdims=True)
        p = jnp.exp2(sa - m)
        invs.append(1.0 / jnp.sum(p, axis=-1, keepdims=True))
        ps.append(p.astype(BF16))
    return jnp.concatenate(ps, axis=0), invs


def _b_values(p, v, invs, tq):
    o = jnp.dot(p, v, preferred_element_type=F32)
    ob = o[:tq] * invs[0]
    for a in range(1, SLOTS):
        ob = jnp.where(_slot_mask(a), o[a * tq:(a + 1) * tq] * invs[a], ob)
    return ob


def _attend(qs, k, v, tq, biases):
    p, invs = _b_softmax(_b_scores(qs, k), tq, biases)
    return _b_values(p, v, invs, tq)


def _a_scores(q, k, t, tq, window):
    qs = _stack_masked([q[:, j * BLK:(j + 1) * BLK] for j in range(A_GROUPS)], [t] * A_GROUPS)
    st = lax.dot_general(k, qs, NT, preferred_element_type=F32)
    if window is None:
        return [st]
    return [st[:tq] + window[0], st[tq:2 * tq], st[2 * tq:3 * tq] + window[1], st[3 * tq:]]


def _a_softmax(parts, sinks_t, tq):
    nq = A_GROUPS * tq
    group = lax.broadcasted_iota(jnp.int32, (1, nq), 1) // tq
    sink = jnp.full((1, nq), sinks_t[A_GROUPS - 1], F32)
    for j in range(A_GROUPS - 1):
        sink = jnp.where(group == j, sinks_t[j], sink)
    m = sink
    for p in parts:
        m = jnp.maximum(m, jnp.max(p, axis=0, keepdims=True))
    pt = jnp.concatenate([jnp.exp2(p - m).astype(BF16) for p in parts], axis=0)
    return pt, jnp.exp2(sink - m)


def _with_ones(vt, t):
    ones = jnp.ones((ONES_ROWS, vt.shape[1]), BF16)
    return jnp.concatenate([vt[t * HEAD_DIM:(t + 1) * HEAD_DIM, :], ones], axis=0)


def _a_values(vt_ones, pt, sink_term):
    ot = jnp.dot(vt_ones, pt, preferred_element_type=F32)
    l = ot[HEAD_DIM:HEAD_DIM + 1] + sink_term
    return ot[:HEAD_DIM] * (1.0 / l)


def _a_assemble(ots, tq):
    blocks = []
    for j in range(A_GROUPS):
        otj = jnp.concatenate([ots[t][:, j * tq:(j + 1) * tq] for t in range(A_KV_HEADS)], axis=0)
        blocks.append(otj.T)
    return jnp.concatenate(blocks, axis=1)


def _attend_t(q, k, vt, tq, window, sinks):
    heads = range(A_KV_HEADS)
    sc = [_a_scores(q, k, t, tq, window) for t in heads]
    ots = [_a_values(_with_ones(vt, t), *_a_softmax(sc[t], sinks[t], tq)) for t in heads]
    return _a_assemble(ots, tq)


def _gate_out_norm(o, g, h, gate, wout_ref, lng, lnb):
    gated = (o * _silu(g.astype(F32))).astype(BF16)
    y = jnp.dot(gated, wout_ref[...], preferred_element_type=F32)
    r = DEEPNORM_ALPHA * h + gate * y
    mu = jnp.mean(r, axis=-1, keepdims=True)
    rc = r - mu
    var = jnp.mean(rc * rc, axis=-1, keepdims=True)
    return rc * lax.rsqrt(var + LN_EPS) * lng + lnb


def _read_sinks(sink_ref):
    return [[sink_ref[t * A_GROUPS + j] * LOG2E for j in range(A_GROUPS)] for t in range(A_KV_HEADS)]


def _attn_a_kernel(sink_ref, q_ref, g_ref, kp_ref, kc_ref, kn_ref, vp_ref, vc_ref, vn_ref,
                   kx_ref, vx_ref, h_ref, gate_ref, wout_ref, lng_ref, lnb_ref, o_ref):
    tq = A_TQ
    i = pl.program_id(1)
    last = pl.num_programs(1) * A_SUBS - 1
    k_lat = jnp.concatenate([kp_ref[...], kc_ref[...], kn_ref[...]], axis=0)
    vt_lat = jnp.concatenate([vp_ref[...], vc_ref[...], vn_ref[...]], axis=1)
    kx, vx = kx_ref[...], vx_ref[...]
    sinks = _read_sinks(sink_ref)
    key = lax.broadcasted_iota(jnp.int32, (tq, tq), 0)
    qry = lax.broadcasted_iota(jnp.int32, (tq, tq), 1)
    heads = range(A_KV_HEADS)
    subs = range(A_SUBS)
    scores = {}
    for sub in subs:
        gi = i * A_SUBS + sub
        bias_prev = jnp.where((key >= qry) & (gi > 0), 0.0, NEG_INF).astype(F32)
        bias_next = jnp.where((key <= qry) & (gi < last), 0.0, NEG_INF).astype(F32)
        window = (jnp.concatenate([bias_prev] * A_GROUPS, axis=1),
                  jnp.concatenate([bias_next] * A_GROUPS, axis=1))
        k = jnp.concatenate([k_lat[sub * tq:(sub + 3) * tq], kx], axis=0)
        q = q_ref[sub * tq:(sub + 1) * tq, :]
        for t in heads:
            scores[sub, t] = _a_scores(q, k, t, tq, window)
    outs = []
    for sub in subs:
        vt = jnp.concatenate([vt_lat[:, sub * tq:(sub + 3) * tq], vx], axis=1)
        ots = [_a_values(_with_ones(vt, t), *_a_softmax(scores[sub, t], sinks[t], tq)) for t in heads]
        outs.append(_a_assemble(ots, tq))
    o_ref[...] = _gate_out_norm(jnp.concatenate(outs, axis=0), g_ref[...], h_ref[...], gate_ref[...],
                                wout_ref, lng_ref[...], lnb_ref[...])


def _attn_a(p, vt, pc, vtc, h, mod, layer, wout, sink, lng, lnb):
    bsz, seq, _ = h.shape
    n_ctx = pc.shape[1]
    tq = A_TQ
    step = A_SUBS * tq
    nb = seq // step
    nsub = seq // tq
    kcol = 2 * D_MODEL // A_KV_DIM
    in_specs = [
        pl.BlockSpec(memory_space=pltpu.SMEM),
        pl.BlockSpec((None, step, D_MODEL), lambda b, i: (b, i, 0)),
        pl.BlockSpec((None, step, D_MODEL), lambda b, i: (b, i, 1)),
        pl.BlockSpec((None, tq, A_KV_DIM), lambda b, i: (b, jnp.maximum(i * A_SUBS - 1, 0), kcol)),
        pl.BlockSpec((None, step, A_KV_DIM), lambda b, i: (b, i, kcol)),
        pl.BlockSpec((None, tq, A_KV_DIM), lambda b, i: (b, jnp.minimum((i + 1) * A_SUBS, nsub - 1), kcol)),
        pl.BlockSpec((None, A_KV_DIM, tq), lambda b, i: (b, 0, jnp.maximum(i * A_SUBS - 1, 0))),
        pl.BlockSpec((None, A_KV_DIM, step), lambda b, i: (b, 0, i)),
        pl.BlockSpec((None, A_KV_DIM, tq), lambda b, i: (b, 0, jnp.minimum((i + 1) * A_SUBS, nsub - 1))),
        pl.BlockSpec((None, n_ctx, A_KV_DIM), lambda b, i: (b, 0, kcol)),
        pl.BlockSpec((None, A_KV_DIM, n_ctx), lambda b, i: (b, 0, 0)),
        pl.BlockSpec((None, step, D_MODEL), lambda b, i: (b, i, 0)),
        pl.BlockSpec((None, None, None, 1, D_MODEL), lambda b, i: (layer, 2, b, 0, 0)),
        pl.BlockSpec((D_MODEL, D_MODEL), lambda b, i: (0, 0)),
        pl.BlockSpec((None, 1, D_MODEL), lambda b, i: (layer, 0, 0)),
        pl.BlockSpec((None, 1, D_MODEL), lambda b, i: (layer, 0, 0)),
    ]
    return pl.pallas_call(
        _attn_a_kernel,
        grid=(bsz, nb),
        in_specs=in_specs,
        out_specs=pl.BlockSpec((None, step, D_MODEL), lambda b, i: (b, i, 0)),
        out_shape=jax.ShapeDtypeStruct((bsz, seq, D_MODEL), F32),
        compiler_params=_params(2),
        name="attn_a",
    )(sink, p, p, p, p, p, vt, vt, vt, pc, vtc, h, mod, wout, lng, lnb)


def _attn_b_kernel(q_ref, k_ref, v_ref, g_ref, kx_ref, vx_ref, bias_ref, h_ref, gate_ref,
                   wout_ref, lng_ref, lnb_ref, o_ref, *, grid_rows):
    band = B_KH * GRID_W
    outs = []
    for rr in range(B_ROWS):
        r = pl.program_id(1) * B_ROWS + rr
        first = jnp.clip(r - B_KH // 2, 0, grid_rows - B_KH)
        start = pl.multiple_of(first * GRID_W, GRID_W)
        dr0 = first - r + (B_KH - 1)
        rows = slice(rr * GRID_W, (rr + 1) * GRID_W)
        scores = []
        for blk in range(N_BLKS):
            cols = slice(blk * BLK, (blk + 1) * BLK)
            qs = _stack_masked([q_ref[rows, cols]] * SLOTS, list(range(SLOTS)))
            k = jnp.concatenate([k_ref[pl.ds(start, band), cols], kx_ref[:, cols]], axis=0)
            scores.append(_b_scores(qs, k))
        blocks = []
        for blk in range(N_BLKS):
            cols = slice(blk * BLK, (blk + 1) * BLK)
            biases = [jnp.concatenate([bias_ref[blk * SLOTS + s, dr0 + 2 * pr] for pr in range(N_PAIR)], axis=1)
                      for s in range(SLOTS)]
            p, invs = _b_softmax(scores[blk], GRID_W, biases)
            v = jnp.concatenate([v_ref[pl.ds(start, band), cols], vx_ref[:, cols]], axis=0)
            blocks.append(_b_values(p, v, invs, GRID_W))
        outs.append(jnp.concatenate(blocks, axis=1))
    o_ref[...] = _gate_out_norm(jnp.concatenate(outs, axis=0), g_ref[...], h_ref[...], gate_ref[...],
                                wout_ref, lng_ref[...], lnb_ref[...])


def _attn_b(p, pc, h, mod, layer, wout, bias_tab, lng, lnb):
    bsz, seq, _ = h.shape
    n_ctx = pc.shape[1]
    grid_rows = seq // GRID_W
    tq = B_ROWS * GRID_W
    in_specs = [
        pl.BlockSpec((None, tq, D_MODEL), lambda b, r: (b, r, 0)),
        pl.BlockSpec((None, seq, D_MODEL), lambda b, r: (b, 0, 1)),
        pl.BlockSpec((None, seq, D_MODEL), lambda b, r: (b, 0, 2)),
        pl.BlockSpec((None, tq, D_MODEL), lambda b, r: (b, r, 3)),
        pl.BlockSpec((None, n_ctx, D_MODEL), lambda b, r: (b, 0, 1)),
        pl.BlockSpec((None, n_ctx, D_MODEL), lambda b, r: (b, 0, 2)),
        pl.BlockSpec(bias_tab.shape, lambda b, r: (0, 0, 0, 0)),
        pl.BlockSpec((None, tq, D_MODEL), lambda b, r: (b, r, 0)),
        pl.BlockSpec((None, None, None, 1, D_MODEL), lambda b, r: (layer, 2, b, 0, 0)),
        pl.BlockSpec((D_MODEL, D_MODEL), lambda b, r: (0, 0)),
        pl.BlockSpec((None, 1, D_MODEL), lambda b, r: (layer, 0, 0)),
        pl.BlockSpec((None, 1, D_MODEL), lambda b, r: (layer, 0, 0)),
    ]
    return pl.pallas_call(
        functools.partial(_attn_b_kernel, grid_rows=grid_rows),
        grid=(bsz, grid_rows // B_ROWS),
        in_specs=in_specs,
        out_specs=pl.BlockSpec((None, tq, D_MODEL), lambda b, r: (b, r, 0)),
        out_shape=jax.ShapeDtypeStruct((bsz, seq, D_MODEL), F32),
        compiler_params=_params(2),
        name="attn_b",
    )(p, p, p, p, pc, pc, bias_tab, h, mod, wout, lng, lnb)


def _ctx_attn_kernel(*refs, mixer_a):
    if mixer_a:
        sink_ref, q_ref, g_ref, k_ref, v_ref, h_ref, gate_ref, wout_ref, lng_ref, lnb_ref, o_ref = refs
        o = _attend_t(q_ref[...], k_ref[...], v_ref[...], q_ref.shape[0], None, _read_sinks(sink_ref))
    else:
        q_ref, g_ref, k_ref, v_ref, h_ref, gate_ref, wout_ref, lng_ref, lnb_ref, o_ref = refs
        n_ctx = q_ref.shape[0]
        q = q_ref[...]
        blocks = []
        for blk in range(N_BLKS):
            cols = slice(blk * BLK, (blk + 1) * BLK)
            qs = _stack_masked([q[:, cols]] * SLOTS, list(range(SLOTS)))
            blocks.append(_attend(qs, k_ref[:, cols], v_ref[:, cols], n_ctx, [None] * SLOTS))
        o = jnp.concatenate(blocks, axis=1)
    o_ref[...] = _gate_out_norm(o, g_ref[...], h_ref[...], gate_ref[...], wout_ref,
                                lng_ref[...], lnb_ref[...])


def _ctx_attn(pc, vtc, hc, mod, layer, wout, lng, lnb, *, mod_row, mixer_a, sink=None):
    bsz, n_ctx, _ = hc.shape
    if mixer_a:
        kcol = 2 * D_MODEL // A_KV_DIM
        qgkv = [
            pl.BlockSpec((None, n_ctx, D_MODEL), lambda b: (b, 0, 0)),
            pl.BlockSpec((None, n_ctx, D_MODEL), lambda b: (b, 0, 1)),
            pl.BlockSpec((None, n_ctx, A_KV_DIM), lambda b: (b, 0, kcol)),
            pl.BlockSpec((None, A_KV_DIM, n_ctx), lambda b: (b, 0, 0)),
        ]
        args = [pc, pc, pc, vtc]
    else:
        qgkv = [
            pl.BlockSpec((None, n_ctx, D_MODEL), lambda b: (b, 0, 0)),
            pl.BlockSpec((None, n_ctx, D_MODEL), lambda b: (b, 0, 3)),
            pl.BlockSpec((None, n_ctx, D_MODEL), lambda b: (b, 0, 1)),
            pl.BlockSpec((None, n_ctx, D_MODEL), lambda b: (b, 0, 2)),
        ]
        args = [pc, pc, pc, pc]
    in_specs = qgkv + [
        pl.BlockSpec((None, n_ctx, D_MODEL), lambda b: (b, 0, 0)),
        pl.BlockSpec((None, None, None, 1, D_MODEL), lambda b: (layer, 2, mod_row, 0, 0)),
        pl.BlockSpec((D_MODEL, D_MODEL), lambda b: (0, 0)),
        pl.BlockSpec((None, 1, D_MODEL), lambda b: (layer, 0, 0)),
        pl.BlockSpec((None, 1, D_MODEL), lambda b: (layer, 0, 0)),
    ]
    args += [hc, mod, wout, lng, lnb]
    if mixer_a:
        in_specs = [pl.BlockSpec(memory_space=pltpu.SMEM)] + in_specs
        args = [sink] + args
    return pl.pallas_call(
        functools.partial(_ctx_attn_kernel, mixer_a=mixer_a),
        grid=(bsz,),
        in_specs=in_specs,
        out_specs=pl.BlockSpec((None, n_ctx, D_MODEL), lambda b: (b, 0, 0)),
        out_shape=jax.ShapeDtypeStruct((bsz, n_ctx, D_MODEL), F32),
        compiler_params=_params(1),
        name="ctx_attn",
    )(*args)


def _head_perm():
    j, t, d = np.meshgrid(np.arange(A_GROUPS), np.arange(A_KV_HEADS), np.arange(HEAD_DIM), indexing="ij")
    return ((t * A_GROUPS + j) * HEAD_DIM + d).reshape(-1)


def _prep_a(w_in, w_out):
    perm = _head_perm()
    q = w_in[:, :D_MODEL][:, perm] * Q_SCALE
    k = w_in[:, D_MODEL:D_MODEL + A_KV_DIM]
    v = w_in[:, D_MODEL + A_KV_DIM:D_MODEL + 2 * A_KV_DIM]
    g = w_in[:, D_MODEL + 2 * A_KV_DIM:][:, perm]
    return (jnp.concatenate([q, g, k], axis=1).astype(BF16), v.T.astype(BF16),
            w_out[perm, :].astype(BF16))


def _prep_b(w_in, w_out):
    w = jnp.concatenate([w_in[:, :D_MODEL] * Q_SCALE, w_in[:, D_MODEL:]], axis=1)
    return w.astype(BF16), w_out.astype(BF16)


def _rope_tables(seq):
    t = jnp.arange(seq)
    row = (t // GRID_W).astype(F32)
    col = (t % GRID_W).astype(F32)
    n_freq = HEAD_DIM // 4
    inv = ROPE_BASE ** (-jnp.arange(n_freq, dtype=F32) / n_freq)
    ang_r = row[:, None] * inv[None]
    ang_c = col[:, None] * inv[None]
    zero = jnp.zeros_like(ang_r)
    cos = jnp.concatenate([jnp.cos(ang_r)] * 2 + [jnp.cos(ang_c)] * 2, axis=1)
    sina = jnp.concatenate([-jnp.sin(ang_r), zero, -jnp.sin(ang_c), zero], axis=1)
    sinb = jnp.concatenate([zero, jnp.sin(ang_r), zero, jnp.sin(ang_c)], axis=1)
    rep = LANES // HEAD_DIM
    return tuple(jnp.tile(x, (1, rep)) for x in (cos, sina, sinb))


def _bias_tab_kernel(rb_ref, o_ref):
    shape = (GRID_W, LANES)
    qc = lax.broadcasted_iota(jnp.int32, shape, 0)
    lane = lax.broadcasted_iota(jnp.int32, shape, 1)
    kc = lane % GRID_W
    cs = jnp.clip(qc - B_KW // 2, 0, GRID_W - B_KW)
    valid = (kc >= cs) & (kc < cs + B_KW)
    left = lane < GRID_W
    sh_left = LANES - (B_KW - 1)
    sh_right = (sh_left + GRID_W) % LANES
    rolled = []
    for dr in range(N_DR):
        row = jnp.broadcast_to(rb_ref[dr:dr + 1, :], shape)
        rolled.append((pltpu.roll(row, sh_left, 1, stride=1, stride_axis=0),
                       pltpu.roll(row, sh_right, 1, stride=1, stride_axis=0)))
    for d in range(N_DR - 1):
        tile = jnp.where(left, rolled[d][0], rolled[d + 1][1])
        o_ref[d] = jnp.where(valid, tile * LOG2E, NEG_INF)


def _b_bias_table(rel_bias):
    n_dc = rel_bias.shape[2]
    rb = jnp.pad(rel_bias.astype(F32), ((0, 0), (0, 0), (0, LANES - n_dc)))
    return pl.pallas_call(
        _bias_tab_kernel,
        grid=(N_HEADS,),
        in_specs=[pl.BlockSpec((None, N_DR, LANES), lambda hd: (hd, 0, 0))],
        out_specs=pl.BlockSpec((None, N_DR - 1, GRID_W, LANES), lambda hd: (hd, 0, 0, 0)),
        out_shape=jax.ShapeDtypeStruct((N_HEADS, N_DR - 1, GRID_W, LANES), F32),
        compiler_params=_params(1),
        name="bias_tab",
    )(rb)


def kernel(x, c, ctx, c_ctx, w_ada, b_ada, ln_g, ln_b, a_w_in, a_w_out, a_sink, b_w_in, b_w_out, b_rel_bias):
    bsz, seq, _ = x.shape
    n_ctx = ctx.shape[1]
    assert bsz + 1 <= MOD_ROWS
    cc = jnp.concatenate([c, c_ctx[None, :], jnp.zeros((MOD_ROWS - bsz - 1, D_MODEL), F32)], axis=0)
    mod = _ada(cc, w_ada, b_ada)
    rope = _rope_tables(seq)
    lng = ln_g.reshape(DEPTH, 1, D_MODEL)
    lnb = ln_b.reshape(DEPTH, 1, D_MODEL)
    rope_chunks = tuple(range(N_BLKS)) + (2 * N_BLKS,)
    h, hc = x, ctx
    for i in range(DEPTH):
        j = i // 2
        ctx_out = i < DEPTH - 1
        if i % 2 == 0:
            w_in, w_vt, w_out = _prep_a(a_w_in[j], a_w_out[j])
            p, vt = _inproj(h, mod, i, w_in, tm=512, mod_row=None, w_vt=w_vt,
                            rope_tabs=rope, rope_chunks=rope_chunks)
            pc, vtc = _inproj(hc, mod, i, w_in, tm=n_ctx, mod_row=bsz, w_vt=w_vt)
            h = _attn_a(p, vt, pc, vtc, h, mod, i, w_out, a_sink[j], lng, lnb)
            if ctx_out:
                hc = _ctx_attn(pc, vtc, hc, mod, i, w_out, lng, lnb, mod_row=bsz, mixer_a=True,
                               sink=a_sink[j])
        else:
            w_in, w_out = _prep_b(b_w_in[j], b_w_out[j])
            (p,) = _inproj(h, mod, i, w_in, tm=512, mod_row=None)
            (pc,) = _inproj(hc, mod, i, w_in, tm=n_ctx, mod_row=bsz)
            h = _attn_b(p, pc, h, mod, i, w_out, _b_bias_table(b_rel_bias[j]), lng, lnb)
            if ctx_out:
                hc = _ctx_attn(pc, None, hc, mod, i, w_out, lng, lnb, mod_row=bsz, mixer_a=False)
    return h
```

```python
import functools
import math

import jax
import jax.numpy as jnp
from jax import lax
from jax.experimental import pallas as pl
from jax.experimental.pallas import tpu as pltpu

D_MODEL = 1024
DEPTH = 4
GRID_W = 64
HEAD_DIM = 64
N_HEADS = 16
A_KV_HEADS = 4
A_GROUPS = 4
A_WINDOW = 128
A_KV_DIM = A_KV_HEADS * HEAD_DIM
B_KH = 8
B_KW = 16
ROPE_BASE = 10000.0
LN_EPS = 1e-5
NEG_INF = -1e30
DEEPNORM_ALPHA = (2.0 * DEPTH) ** 0.25
Q_SCALE = HEAD_DIM ** -0.5
LOG2E = math.log2(math.e)

LANES = 128
BLK = 256
SLOTS = BLK // HEAD_DIM
N_BLKS = D_MODEL // BLK
MOD_ROWS = 40
VMEM_LIMIT = 56 * 1024 * 1024
PROJ_ROWS = 1024
A_TQ = 128
A_SUBS = 4
ONES_ROWS = 16
B_ROWS = 4
N_DR = 2 * B_KH - 1
N_PAIR = B_KH // 2

assert A_TQ == A_WINDOW

F32 = jnp.float32
BF16 = jnp.bfloat16
NT = (((1,), (1,)), ((), ()))


def _silu(x):
    return x / (1.0 + jnp.exp(-x))


def _slot_mask(slot):
    lane = lax.broadcasted_iota(jnp.int32, (1, BLK), 1)
    return (lane // HEAD_DIM) == slot


def _params(n_axes):
    return pltpu.CompilerParams(
        dimension_semantics=("arbitrary",) * n_axes, vmem_limit_bytes=VMEM_LIMIT)


def _ada_kernel(c_ref, w_ref, b_ref, o_ref):
    sc = _silu(c_ref[...])
    o_ref[...] = jnp.dot(sc, w_ref[...], preferred_element_type=F32,
                         precision=lax.Precision.HIGHEST) + b_ref[...]


def _ada(cc, w_ada, b_ada):
    out = pl.pallas_call(
        _ada_kernel,
        grid=(DEPTH, 3),
        in_specs=[
            pl.BlockSpec((MOD_ROWS, D_MODEL), lambda d, j: (0, 0)),
            pl.BlockSpec((None, D_MODEL, D_MODEL), lambda d, j: (d, 0, j)),
            pl.BlockSpec((None, None, 1, D_MODEL), lambda d, j: (d, j, 0, 0)),
        ],
        out_specs=pl.BlockSpec((None, None, MOD_ROWS, D_MODEL), lambda d, j: (d, j, 0, 0)),
        out_shape=jax.ShapeDtypeStruct((DEPTH, 3, MOD_ROWS, D_MODEL), F32),
        compiler_params=_params(2),
        name="ada_mod",
    )(cc, w_ada, b_ada.reshape(DEPTH, 3, 1, D_MODEL))
    return out.reshape(DEPTH, 3, MOD_ROWS, 1, D_MODEL)


def _inproj_kernel(*refs, n_out, q_chunks, rope_chunks, with_vt):
    refs = list(refs)
    x_ref, shift_ref, scale_ref, w_ref = refs[:4]
    refs = refs[4:]
    if with_vt:
        wvt_ref = refs.pop(0)
    if rope_chunks:
        cos_ref, sina_ref, sinb_ref = refs[:3]
        refs = refs[3:]
    o_ref = refs[0]
    u = (x_ref[...] * (1.0 + scale_ref[...]) + shift_ref[...]).astype(BF16)
    for ci in range(n_out // BLK):
        c0 = ci * BLK
        acc = jnp.dot(u, w_ref[:, c0:c0 + BLK], preferred_element_type=F32)
        if ci < q_chunks:
            acc = acc * LOG2E
        if ci in rope_chunks:
            cos, sina, sinb = cos_ref[...], sina_ref[...], sinb_ref[...]
            parts = []
            for l0 in range(0, BLK, LANES):
                a = acc[:, l0:l0 + LANES]
                parts.append(a * cos + pltpu.roll(a, LANES - 16, 1) * sina + pltpu.roll(a, 16, 1) * sinb)
            acc = jnp.concatenate(parts, axis=1)
        o_ref[:, c0:c0 + BLK] = acc.astype(BF16)
    if with_vt:
        vt_ref = refs[1]
        vt_ref[...] = lax.dot_general(wvt_ref[...], u, NT, preferred_element_type=F32).astype(BF16)


def _inproj(h, mod, layer, w, *, tm, mod_row, q_chunks=N_BLKS, w_vt=None, rope_tabs=None, rope_chunks=()):
    bsz, seq, _ = h.shape
    n_out = w.shape[1]
    row = (lambda b: b) if mod_row is None else (lambda b: mod_row)
    in_specs = [
        pl.BlockSpec((None, tm, D_MODEL), lambda b, s: (b, s, 0)),
        pl.BlockSpec((None, None, None, 1, D_MODEL), lambda b, s: (layer, 0, row(b), 0, 0)),
        pl.BlockSpec((None, None, None, 1, D_MODEL), lambda b, s: (layer, 1, row(b), 0, 0)),
        pl.BlockSpec((D_MODEL, n_out), lambda b, s: (0, 0)),
    ]
    args = [h, mod, mod, w]
    out_specs = [pl.BlockSpec((None, tm, n_out), lambda b, s: (b, s, 0))]
    out_shape = [jax.ShapeDtypeStruct((bsz, seq, n_out), BF16)]
    if w_vt is not None:
        in_specs.append(pl.BlockSpec(w_vt.shape, lambda b, s: (0, 0)))
        args.append(w_vt)
        out_specs.append(pl.BlockSpec((None, w_vt.shape[0], tm), lambda b, s: (b, 0, s)))
        out_shape.append(jax.ShapeDtypeStruct((bsz, w_vt.shape[0], seq), BF16))
    if rope_chunks:
        in_specs += [pl.BlockSpec((tm, LANES), lambda b, s: (s, 0))] * 3
        args += list(rope_tabs)
    return pl.pallas_call(
        functools.partial(_inproj_kernel, n_out=n_out, q_chunks=q_chunks,
                          rope_chunks=tuple(rope_chunks), with_vt=w_vt is not None),
        grid=(bsz, seq // tm),
        in_specs=in_specs,
        out_specs=out_specs,
        out_shape=out_shape,
        compiler_params=_params(2),
        name="inproj",
    )(*args)


def _stack_masked(q_blocks, slots):
    return jnp.concatenate(
        [jnp.where(_slot_mask(s), q, jnp.zeros_like(q)) for q, s in zip(q_blocks, slots)], axis=0)


def _b_scores(qs, k):
    return lax.dot_general(qs, k, NT, preferred_element_type=F32)


def _b_softmax(s, tq, biases):
    ps, invs = [], []
    for a in range(SLOTS):
        sa = s[a * tq:(a + 1) * tq]
        if biases[a] is not None:
            nb = biases[a].shape[1]
            sa = jnp.concatenate([sa[:, :nb] + biases[a], sa[:, nb:]], axis=1)
        m = jnp.max(sa, axis=-1, keepdims=True)
        p = jnp.exp2(sa - m)
        invs.append(1.0 / jnp.sum(p, axis=-1, keepdims=True))
        ps.append(p.astype(BF16))
    return jnp.concatenate(ps, axis=0), invs


def _b_values(p, v, invs, tq):
    o = jnp.dot(p, v, preferred_element_type=F32)
    ob = o[:tq] * invs[0]
    for a in range(1, SLOTS):
        ob = jnp.where(_slot_mask(a), o[a * tq:(a + 1) * tq] * invs[a], ob)
    return ob


def _attend(qs, k, v, tq, biases):
    p, invs = _b_softmax(_b_scores(qs, k), tq, biases)
    return _b_values(p, v, invs, tq)


def _a_scores(q, k, t, tq, window):
    qs = _stack_masked([q[:, j * BLK:(j + 1) * BLK] for j in range(A_GROUPS)], [t] * A_GROUPS)
    st = lax.dot_general(k, qs, NT, preferred_element_type=F32)
    if window is None:
        return [st]
    return [st[:tq] + window[0], st[tq:2 * tq], st[2 * tq:3 * tq] + window[1], st[3 * tq:]]


def _a_softmax(parts, sinks_t, tq):
    nq = A_GROUPS * tq
    group = lax.broadcasted_iota(jnp.int32, (1, nq), 1) // tq
    sink = jnp.full((1, nq), sinks_t[A_GROUPS - 1], F32)
    for j in range(A_GROUPS - 1):
        sink = jnp.where(group == j, sinks_t[j], sink)
    m = sink
    for p in parts:
        m = jnp.maximum(m, jnp.max(p, axis=0, keepdims=True))
    pt = jnp.concatenate([jnp.exp2(p - m).astype(BF16) for p in parts], axis=0)
    return pt, jnp.exp2(sink - m)


def _with_ones(vt, t):
    ones = jnp.ones((ONES_ROWS, vt.shape[1]), BF16)
    return jnp.concatenate([vt[t * HEAD_DIM:(t + 1) * HEAD_DIM, :], ones], axis=0)


def _a_values(vt_ones, pt, sink_term):
    ot = jnp.dot(vt_ones, pt, preferred_element_type=F32)
    l = ot[HEAD_DIM:HEAD_DIM + 1] + sink_term
    return ot[:HEAD_DIM] * (1.0 / l)


def _a_assemble(ots, tq):
    blocks = []
    for j in range(A_GROUPS):
        otj = jnp.concatenate([ots[t][:, j * tq:(j + 1) * tq] for t in range(A_KV_HEADS)], axis=0)
        blocks.append(otj.T)
    return jnp.concatenate(blocks, axis=1)


def _attend_t(q, k, vt, tq, window, sinks):
    heads = range(A_KV_HEADS)
    sc = [_a_scores(q, k, t, tq, window) for t in heads]
    ots = [_a_values(_with_ones(vt, t), *_a_softmax(sc[t], sinks[t], tq)) for t in heads]
    return _a_assemble(ots, tq)


def _gate_out_norm(o, g, h, gate, wout_ref, lng, lnb):
    gated = (o * _silu(g.astype(F32))).astype(BF16)
    y = jnp.dot(gated, wout_ref[...], preferred_element_type=F32)
    r = DEEPNORM_ALPHA * h + gate * y
    mu = jnp.mean(r, axis=-1, keepdims=True)
    rc = r - mu
    var = jnp.mean(rc * rc, axis=-1, keepdims=True)
    return rc * lax.rsqrt(var + LN_EPS) * lng + lnb


def _read_sinks(sink_ref):
    return [[sink_ref[t * A_GROUPS + j] * LOG2E for j in range(A_GROUPS)] for t in range(A_KV_HEADS)]


def _attn_a_kernel(sink_ref, q_ref, g_ref, kp_ref, kc_ref, kn_ref, vp_ref, vc_ref, vn_ref,
                   kx_ref, vx_ref, h_ref, gate_ref, wout_ref, lng_ref, lnb_ref, o_ref):
    tq = A_TQ
    i = pl.program_id(1)
    last = pl.num_programs(1) * A_SUBS - 1
    k_lat = jnp.concatenate([kp_ref[...], kc_ref[...], kn_ref[...]], axis=0)
    vt_lat = jnp.concatenate([vp_ref[...], vc_ref[...], vn_ref[...]], axis=1)
    kx, vx = kx_ref[...], vx_ref[...]
    sinks = _read_sinks(sink_ref)
    key = lax.broadcasted_iota(jnp.int32, (tq, tq), 0)
    qry = lax.broadcasted_iota(jnp.int32, (tq, tq), 1)
    heads = range(A_KV_HEADS)
    subs = range(A_SUBS)
    scores = {}
    for sub in subs:
        gi = i * A_SUBS + sub
        bias_prev = jnp.where((key >= qry) & (gi > 0), 0.0, NEG_INF).astype(F32)
        bias_next = jnp.where((key <= qry) & (gi < last), 0.0, NEG_INF).astype(F32)
        window = (jnp.concatenate([bias_prev] * A_GROUPS, axis=1),
                  jnp.concatenate([bias_next] * A_GROUPS, axis=1))
        k = jnp.concatenate([k_lat[sub * tq:(sub + 3) * tq], kx], axis=0)
        q = q_ref[sub * tq:(sub + 1) * tq, :]
        for t in heads:
            scores[sub, t] = _a_scores(q, k, t, tq, window)
    outs = []
    for sub in subs:
        vt = jnp.concatenate([vt_lat[:, sub * tq:(sub + 3) * tq], vx], axis=1)
        ots = [_a_values(_with_ones(vt, t), *_a_softmax(scores[sub, t], sinks[t], tq)) for t in heads]
        outs.append(_a_assemble(ots, tq))
    o_ref[...] = _gate_out_norm(jnp.concatenate(outs, axis=0), g_ref[...], h_ref[...], gate_ref[...],
                                wout_ref, lng_ref[...], lnb_ref[...])


def _attn_a(p, vt, pc, vtc, h, mod, layer, wout, sink, lng, lnb):
    bsz, seq, _ = h.shape
    n_ctx = pc.shape[1]
    tq = A_TQ
    step = A_SUBS * tq
    nb = seq // step
    nsub = seq // tq
    kcol = 2 * D_MODEL // A_KV_DIM
    in_specs = [
        pl.BlockSpec(memory_space=pltpu.SMEM),
        pl.BlockSpec((None, step, D_MODEL), lambda b, i: (b, i, 0)),
        pl.BlockSpec((None, step, D_MODEL), lambda b, i: (b, i, 1)),
        pl.BlockSpec((None, tq, A_KV_DIM), lambda b, i: (b, jnp.maximum(i * A_SUBS - 1, 0), kcol)),
        pl.BlockSpec((None, step, A_KV_DIM), lambda b, i: (b, i, kcol)),
        pl.BlockSpec((None, tq, A_KV_DIM), lambda b, i: (b, jnp.minimum((i + 1) * A_SUBS, nsub - 1), kcol)),
        pl.BlockSpec((None, A_KV_DIM, tq), lambda b, i: (b, 0, jnp.maximum(i * A_SUBS - 1, 0))),
        pl.BlockSpec((None, A_KV_DIM, step), lambda b, i: (b, 0, i)),
        pl.BlockSpec((None, A_KV_DIM, tq), lambda b, i: (b, 0, jnp.minimum((i + 1) * A_SUBS, nsub - 1))),
        pl.BlockSpec((None, n_ctx, A_KV_DIM), lambda b, i: (b, 0, kcol)),
        pl.BlockSpec((None, A_KV_DIM, n_ctx), lambda b, i: (b, 0, 0)),
        pl.BlockSpec((None, step, D_MODEL), lambda b, i: (b, i, 0)),
        pl.BlockSpec((None, None, None, 1, D_MODEL), lambda b, i: (layer, 2, b, 0, 0)),
        pl.BlockSpec((D_MODEL, D_MODEL), lambda b, i: (0, 0)),
        pl.BlockSpec((None, 1, D_MODEL), lambda b, i: (layer, 0, 0)),
        pl.BlockSpec((None, 1, D_MODEL), lambda b, i: (layer, 0, 0)),
    ]
    return pl.pallas_call(
        _attn_a_kernel,
        grid=(bsz, nb),
        in_specs=in_specs,
        out_specs=pl.BlockSpec((None, step, D_MODEL), lambda b, i: (b, i, 0)),
        out_shape=jax.ShapeDtypeStruct((bsz, seq, D_MODEL), F32),
        compiler_params=_params(2),
        name="attn_a",
    )(sink, p, p, p, p, p, vt, vt, vt, pc, vtc, h, mod, wout, lng, lnb)


def _attn_b_kernel(q_ref, k_ref, v_ref, g_ref, kx_ref, vx_ref, bias_ref, h_ref, gate_ref,
                   wout_ref, lng_ref, lnb_ref, o_ref, s_ref, p_ref, *, grid_rows):
    band = B_KH * GRID_W
    tq = GRID_W
    outs = []
    for rr in range(B_ROWS):
        r = pl.program_id(1) * B_ROWS + rr
        first = jnp.clip(r - B_KH // 2, 0, grid_rows - B_KH)
        start = pl.multiple_of(first * GRID_W, GRID_W)
        dr0 = first - r + (B_KH - 1)
        rows = slice(rr * GRID_W, (rr + 1) * GRID_W)
        for blk in range(N_BLKS):
            cols = slice(blk * BLK, (blk + 1) * BLK)
            qs = _stack_masked([q_ref[rows, cols]] * SLOTS, list(range(SLOTS)))
            k = jnp.concatenate([k_ref[pl.ds(start, band), cols], kx_ref[:, cols]], axis=0)
            s_ref[blk] = _b_scores(qs, k)
        blocks = []
        for blk in range(N_BLKS):
            cols = slice(blk * BLK, (blk + 1) * BLK)
            invs = []
            for a in range(SLOTS):
                grp = slice(a * tq, (a + 1) * tq)
                bias = jnp.concatenate([bias_ref[blk * SLOTS + a, dr0 + 2 * pr] for pr in range(N_PAIR)], axis=1)
                s_lat = s_ref[blk, grp, :band] + bias
                s_ctx = s_ref[blk, grp, band:]
                m = jnp.maximum(jnp.max(s_lat, axis=-1, keepdims=True), jnp.max(s_ctx, axis=-1, keepdims=True))
                e_lat = jnp.exp2(s_lat - m)
                e_ctx = jnp.exp2(s_ctx - m)
                invs.append(1.0 / (jnp.sum(e_lat, axis=-1, keepdims=True) + jnp.sum(e_ctx, axis=-1, keepdims=True)))
                p_ref[blk, grp, :band] = e_lat.astype(BF16)
                p_ref[blk, grp, band:] = e_ctx.astype(BF16)
            v = jnp.concatenate([v_ref[pl.ds(start, band), cols], vx_ref[:, cols]], axis=0)
            blocks.append(_b_values(p_ref[blk], v, invs, tq))
        outs.append(jnp.concatenate(blocks, axis=1))
    o_ref[...] = _gate_out_norm(jnp.concatenate(outs, axis=0), g_ref[...], h_ref[...], gate_ref[...],
                                wout_ref, lng_ref[...], lnb_ref[...])


def _attn_b(p, pc, ctx_kcol, h, mod, layer, wout, bias_tab, lng, lnb):
    bsz, seq, _ = h.shape
    n_ctx = pc.shape[1]
    grid_rows = seq // GRID_W
    tq = B_ROWS * GRID_W
    scratch = (N_BLKS, SLOTS * GRID_W, B_KH * GRID_W + n_ctx)
    in_specs = [
        pl.BlockSpec((None, tq, D_MODEL), lambda b, r: (b, r, 0)),
        pl.BlockSpec((None, seq, D_MODEL), lambda b, r: (b, 0, 1)),
        pl.BlockSpec((None, seq, D_MODEL), lambda b, r: (b, 0, 2)),
        pl.BlockSpec((None, tq, D_MODEL), lambda b, r: (b, r, 3)),
        pl.BlockSpec((None, n_ctx, D_MODEL), lambda b, r: (b, 0, ctx_kcol)),
        pl.BlockSpec((None, n_ctx, D_MODEL), lambda b, r: (b, 0, ctx_kcol + 1)),
        pl.BlockSpec(bias_tab.shape, lambda b, r: (0, 0, 0, 0)),
        pl.BlockSpec((None, tq, D_MODEL), lambda b, r: (b, r, 0)),
        pl.BlockSpec((None, None, None, 1, D_MODEL), lambda b, r: (layer, 2, b, 0, 0)),
        pl.BlockSpec((D_MODEL, D_MODEL), lambda b, r: (0, 0)),
        pl.BlockSpec((None, 1, D_MODEL), lambda b, r: (layer, 0, 0)),
        pl.BlockSpec((None, 1, D_MODEL), lambda b, r: (layer, 0, 0)),
    ]
    return pl.pallas_call(
        functools.partial(_attn_b_kernel, grid_rows=grid_rows),
        grid=(bsz, grid_rows // B_ROWS),
        in_specs=in_specs,
        out_specs=pl.BlockSpec((None, tq, D_MODEL), lambda b, r: (b, r, 0)),
        out_shape=jax.ShapeDtypeStruct((bsz, seq, D_MODEL), F32),
        scratch_shapes=[pltpu.VMEM(scratch, F32), pltpu.VMEM(scratch, BF16)],
        compiler_params=_params(2),
        name="attn_b",
    )(p, p, p, p, pc, pc, bias_tab, h, mod, wout, lng, lnb)


def _ctx_attn_kernel(*refs, mixer_a):
    if mixer_a:
        sink_ref, q_ref, g_ref, k_ref, v_ref, h_ref, gate_ref, wout_ref, lng_ref, lnb_ref, o_ref = refs
        o = _attend_t(q_ref[...], k_ref[...], v_ref[...], q_ref.shape[0], None, _read_sinks(sink_ref))
    else:
        q_ref, g_ref, k_ref, v_ref, h_ref, gate_ref, wout_ref, lng_ref, lnb_ref, o_ref = refs
        n_ctx = q_ref.shape[0]
        q = q_ref[...]
        blocks = []
        for blk in range(N_BLKS):
            cols = slice(blk * BLK, (blk + 1) * BLK)
            qs = _stack_masked([q[:, cols]] * SLOTS, list(range(SLOTS)))
            blocks.append(_attend(qs, k_ref[:, cols], v_ref[:, cols], n_ctx, [None] * SLOTS))
        o = jnp.concatenate(blocks, axis=1)
    o_ref[...] = _gate_out_norm(o, g_ref[...], h_ref[...], gate_ref[...], wout_ref,
                                lng_ref[...], lnb_ref[...])


def _ctx_attn(pc, vtc, hc, mod, layer, wout, lng, lnb, *, mod_row, mixer_a, sink=None):
    bsz, n_ctx, _ = hc.shape
    if mixer_a:
        kcol = 2 * D_MODEL // A_KV_DIM
        qgkv = [
            pl.BlockSpec((None, n_ctx, D_MODEL), lambda b: (b, 0, 0)),
            pl.BlockSpec((None, n_ctx, D_MODEL), lambda b: (b, 0, 1)),
            pl.BlockSpec((None, n_ctx, A_KV_DIM), lambda b: (b, 0, kcol)),
            pl.BlockSpec((None, A_KV_DIM, n_ctx), lambda b: (b, 0, 0)),
        ]
        args = [pc, pc, pc, vtc]
    else:
        qgkv = [
            pl.BlockSpec((None, n_ctx, D_MODEL), lambda b: (b, 0, 0)),
            pl.BlockSpec((None, n_ctx, D_MODEL), lambda b: (b, 0, 3)),
            pl.BlockSpec((None, n_ctx, D_MODEL), lambda b: (b, 0, 1)),
            pl.BlockSpec((None, n_ctx, D_MODEL), lambda b: (b, 0, 2)),
        ]
        args = [pc, pc, pc, pc]
    in_specs = qgkv + [
        pl.BlockSpec((None, n_ctx, D_MODEL), lambda b: (b, 0, 0)),
        pl.BlockSpec((None, None, None, 1, D_MODEL), lambda b: (layer, 2, mod_row, 0, 0)),
        pl.BlockSpec((D_MODEL, D_MODEL), lambda b: (0, 0)),
        pl.BlockSpec((None, 1, D_MODEL), lambda b: (layer, 0, 0)),
        pl.BlockSpec((None, 1, D_MODEL), lambda b: (layer, 0, 0)),
    ]
    args += [hc, mod, wout, lng, lnb]
    if mixer_a:
        in_specs = [pl.BlockSpec(memory_space=pltpu.SMEM)] + in_specs
        args = [sink] + args
    return pl.pallas_call(
        functools.partial(_ctx_attn_kernel, mixer_a=mixer_a),
        grid=(bsz,),
        in_specs=in_specs,
        out_specs=pl.BlockSpec((None, n_ctx, D_MODEL), lambda b: (b, 0, 0)),
        out_shape=jax.ShapeDtypeStruct((bsz, n_ctx, D_MODEL), F32),
        compiler_params=_params(1),
        name="ctx_attn",
    )(*args)


def _regroup_heads(w, axis):
    shape = w.shape
    w = w.reshape(shape[:axis] + (A_KV_HEADS, A_GROUPS, HEAD_DIM) + shape[axis + 1:])
    return jnp.swapaxes(w, axis, axis + 1).reshape(shape)


def _prep_a(w_in, w_out):
    q = _regroup_heads(w_in[:, :D_MODEL], 1) * Q_SCALE
    k = w_in[:, D_MODEL:D_MODEL + A_KV_DIM]
    v = w_in[:, D_MODEL + A_KV_DIM:D_MODEL + 2 * A_KV_DIM]
    g = _regroup_heads(w_in[:, D_MODEL + 2 * A_KV_DIM:], 1)
    return (jnp.concatenate([q, g, k], axis=1).astype(BF16), v.T.astype(BF16),
            _regroup_heads(w_out, 0).astype(BF16))


def _prep_b(w_in, w_out):
    w = jnp.concatenate([w_in[:, :D_MODEL] * Q_SCALE, w_in[:, D_MODEL:]], axis=1)
    return w.astype(BF16), w_out.astype(BF16)


def _rope_tables(seq):
    t = jnp.arange(seq)
    row = (t // GRID_W).astype(F32)
    col = (t % GRID_W).astype(F32)
    n_freq = HEAD_DIM // 4
    inv = ROPE_BASE ** (-jnp.arange(n_freq, dtype=F32) / n_freq)
    ang_r = row[:, None] * inv[None]
    ang_c = col[:, None] * inv[None]
    zero = jnp.zeros_like(ang_r)
    cos = jnp.concatenate([jnp.cos(ang_r)] * 2 + [jnp.cos(ang_c)] * 2, axis=1)
    sina = jnp.concatenate([-jnp.sin(ang_r), zero, -jnp.sin(ang_c), zero], axis=1)
    sinb = jnp.concatenate([zero, jnp.sin(ang_r), zero, jnp.sin(ang_c)], axis=1)
    rep = LANES // HEAD_DIM
    return tuple(jnp.tile(x, (1, rep)) for x in (cos, sina, sinb))


def _bias_tab_kernel(rb_ref, o_ref):
    shape = (GRID_W, LANES)
    qc = lax.broadcasted_iota(jnp.int32, shape, 0)
    lane = lax.broadcasted_iota(jnp.int32, shape, 1)
    kc = lane % GRID_W
    cs = jnp.clip(qc - B_KW // 2, 0, GRID_W - B_KW)
    valid = (kc >= cs) & (kc < cs + B_KW)
    left = lane < GRID_W
    sh_left = LANES - (B_KW - 1)
    sh_right = (sh_left + GRID_W) % LANES
    rolled = []
    for dr in range(N_DR):
        row = jnp.broadcast_to(rb_ref[dr:dr + 1, :], shape)
        rolled.append((pltpu.roll(row, sh_left, 1, stride=1, stride_axis=0),
                       pltpu.roll(row, sh_right, 1, stride=1, stride_axis=0)))
    for d in range(N_DR - 1):
        tile = jnp.where(left, rolled[d][0], rolled[d + 1][1])
        o_ref[d] = jnp.where(valid, tile * LOG2E, NEG_INF)


def _b_bias_table(rel_bias):
    n_dc = rel_bias.shape[2]
    rb = jnp.pad(rel_bias.astype(F32), ((0, 0), (0, 0), (0, LANES - n_dc)))
    return pl.pallas_call(
        _bias_tab_kernel,
        grid=(N_HEADS,),
        in_specs=[pl.BlockSpec((None, N_DR, LANES), lambda hd: (hd, 0, 0))],
        out_specs=pl.BlockSpec((None, N_DR - 1, GRID_W, LANES), lambda hd: (hd, 0, 0, 0)),
        out_shape=jax.ShapeDtypeStruct((N_HEADS, N_DR - 1, GRID_W, LANES), F32),
        compiler_params=_params(1),
        name="bias_tab",
    )(rb)


def kernel(x, c, ctx, c_ctx, w_ada, b_ada, ln_g, ln_b, a_w_in, a_w_out, a_sink, b_w_in, b_w_out, b_rel_bias):
    bsz, seq, _ = x.shape
    n_ctx = ctx.shape[1]
    assert bsz + 1 <= MOD_ROWS
    cc = jnp.concatenate([c, c_ctx[None, :], jnp.zeros((MOD_ROWS - bsz - 1, D_MODEL), F32)], axis=0)
    mod = _ada(cc, w_ada, b_ada)
    rope = _rope_tables(seq)
    lng = ln_g.reshape(DEPTH, 1, D_MODEL)
    lnb = ln_b.reshape(DEPTH, 1, D_MODEL)
    rope_chunks = tuple(range(N_BLKS)) + (2 * N_BLKS,)
    h, hc = x, ctx
    for i in range(DEPTH):
        j = i // 2
        ctx_out = i < DEPTH - 1
        if i % 2 == 0:
            w_in, w_vt, w_out = _prep_a(a_w_in[j], a_w_out[j])
            p, vt = _inproj(h, mod, i, w_in, tm=PROJ_ROWS, mod_row=None, w_vt=w_vt,
                            rope_tabs=rope, rope_chunks=rope_chunks)
            pc, vtc = _inproj(hc, mod, i, w_in, tm=n_ctx, mod_row=bsz, w_vt=w_vt)
            h = _attn_a(p, vt, pc, vtc, h, mod, i, w_out, a_sink[j], lng, lnb)
            if ctx_out:
                hc = _ctx_attn(pc, vtc, hc, mod, i, w_out, lng, lnb, mod_row=bsz, mixer_a=True,
                               sink=a_sink[j])
        else:
            w_in, w_out = _prep_b(b_w_in[j], b_w_out[j])
            (p,) = _inproj(h, mod, i, w_in, tm=PROJ_ROWS, mod_row=None)
            if ctx_out:
                (pc,) = _inproj(hc, mod, i, w_in, tm=n_ctx, mod_row=bsz)
                ctx_kcol = 1
            else:
                (pc,) = _inproj(hc, mod, i, w_in[:, D_MODEL:3 * D_MODEL], tm=n_ctx, mod_row=bsz, q_chunks=0)
                ctx_kcol = 0
            h = _attn_b(p, pc, ctx_kcol, h, mod, i, w_out, _b_bias_table(b_rel_bias[j]), lng, lnb)
            if ctx_out:
                hc = _ctx_attn(pc, None, hc, mod, i, w_out, lng, lnb, mod_row=bsz, mixer_a=False)
    return h
```

```python
import functools
import math

import jax
import jax.numpy as jnp
from jax import lax
from jax.experimental import pallas as pl
from jax.experimental.pallas import tpu as pltpu

D_MODEL = 1024
DEPTH = 4
GRID_W = 64
HEAD_DIM = 64
N_HEADS = 16
A_KV_HEADS = 4
A_GROUPS = 4
A_WINDOW = 128
A_KV_DIM = A_KV_HEADS * HEAD_DIM
B_KH = 8
B_KW = 16
ROPE_BASE = 10000.0
LN_EPS = 1e-5
NEG_INF = -1e30
DEEPNORM_ALPHA = (2.0 * DEPTH) ** 0.25
Q_SCALE = HEAD_DIM ** -0.5
LOG2E = math.log2(math.e)

LANES = 128
BLK = 256
SLOTS = BLK // HEAD_DIM
N_BLKS = D_MODEL // BLK
MOD_ROWS = 40
VMEM_LIMIT = 56 * 1024 * 1024
PROJ_ROWS = 1024
A_TQ = 128
A_SUBS = 4
ONES_ROWS = 16
B_ROWS = 4
N_DR = 2 * B_KH - 1
N_PAIR = B_KH // 2

assert A_TQ == A_WINDOW

F32 = jnp.float32
BF16 = jnp.bfloat16
NT = (((1,), (1,)), ((), ()))


def _silu(x):
    return x / (1.0 + jnp.exp(-x))


def _slot_mask(slot):
    lane = lax.broadcasted_iota(jnp.int32, (1, BLK), 1)
    return (lane // HEAD_DIM) == slot


def _params(n_axes):
    return pltpu.CompilerParams(
        dimension_semantics=("arbitrary",) * n_axes, vmem_limit_bytes=VMEM_LIMIT)


def _ada_kernel(c_ref, w_ref, b_ref, o_ref):
    sc = _silu(c_ref[...])
    o_ref[...] = jnp.dot(sc, w_ref[...], preferred_element_type=F32,
                         precision=lax.Precision.HIGHEST) + b_ref[...]


def _ada(cc, w_ada, b_ada):
    out = pl.pallas_call(
        _ada_kernel,
        grid=(DEPTH, 3),
        in_specs=[
            pl.BlockSpec((MOD_ROWS, D_MODEL), lambda d, j: (0, 0)),
            pl.BlockSpec((None, D_MODEL, D_MODEL), lambda d, j: (d, 0, j)),
            pl.BlockSpec((None, None, 1, D_MODEL), lambda d, j: (d, j, 0, 0)),
        ],
        out_specs=pl.BlockSpec((None, None, MOD_ROWS, D_MODEL), lambda d, j: (d, j, 0, 0)),
        out_shape=jax.ShapeDtypeStruct((DEPTH, 3, MOD_ROWS, D_MODEL), F32),
        compiler_params=_params(2),
        name="ada_mod",
    )(cc, w_ada, b_ada.reshape(DEPTH, 3, 1, D_MODEL))
    return out.reshape(DEPTH, 3, MOD_ROWS, 1, D_MODEL)


def _inproj_kernel(*refs, n_out, q_chunks, rope_chunks, n_t, t_rope):
    refs = list(refs)
    x_ref, shift_ref, scale_ref, w_ref = refs[:4]
    refs = refs[4:]
    if n_t:
        wt_ref = refs.pop(0)
    if rope_chunks:
        cos_ref, sina_ref, sinb_ref = refs[:3]
        refs = refs[3:]
    if t_rope:
        cost_ref, sint_ref = refs[:2]
        refs = refs[2:]
    o_ref = refs[0]
    u = (x_ref[...] * (1.0 + scale_ref[...]) + shift_ref[...]).astype(BF16)
    for ci in range(n_out // BLK):
        c0 = ci * BLK
        acc = jnp.dot(u, w_ref[:, c0:c0 + BLK], preferred_element_type=F32)
        if ci < q_chunks:
            acc = acc * LOG2E
        if ci in rope_chunks:
            cos, sina, sinb = cos_ref[...], sina_ref[...], sinb_ref[...]
            parts = []
            for l0 in range(0, BLK, LANES):
                a = acc[:, l0:l0 + LANES]
                parts.append(a * cos + pltpu.roll(a, LANES - 16, 1) * sina + pltpu.roll(a, 16, 1) * sinb)
            acc = jnp.concatenate(parts, axis=1)
        o_ref[:, c0:c0 + BLK] = acc.astype(BF16)
    if n_t:
        t_ref = refs[1]
        pair = HEAD_DIM // 4
        for ci in range(n_t):
            r0 = ci * BLK
            acc = lax.dot_general(wt_ref[r0:r0 + BLK, :], u, NT, preferred_element_type=F32)
            if ci < N_BLKS:
                acc = acc * LOG2E
                if t_rope:
                    cost, sint = cost_ref[...], sint_ref[...]
                    heads = []
                    for r in range(0, BLK, HEAD_DIM):
                        xh = acc[r:r + HEAD_DIM]
                        swapped = jnp.concatenate([xh[pair:2 * pair], xh[:pair], xh[3 * pair:], xh[2 * pair:3 * pair]], axis=0)
                        heads.append(xh * cost + swapped * sint)
                    acc = jnp.concatenate(heads, axis=0)
            t_ref[r0:r0 + BLK, :] = acc.astype(BF16)


def _inproj(h, mod, layer, w, *, tm, mod_row, q_chunks=N_BLKS, w_t=None, rope_tabs=None, rope_chunks=(),
            t_rope_tabs=None):
    bsz, seq, _ = h.shape
    n_out = w.shape[1]
    row = (lambda b: b) if mod_row is None else (lambda b: mod_row)
    in_specs = [
        pl.BlockSpec((None, tm, D_MODEL), lambda b, s: (b, s, 0)),
        pl.BlockSpec((None, None, None, 1, D_MODEL), lambda b, s: (layer, 0, row(b), 0, 0)),
        pl.BlockSpec((None, None, None, 1, D_MODEL), lambda b, s: (layer, 1, row(b), 0, 0)),
        pl.BlockSpec((D_MODEL, n_out), lambda b, s: (0, 0)),
    ]
    args = [h, mod, mod, w]
    out_specs = [pl.BlockSpec((None, tm, n_out), lambda b, s: (b, s, 0))]
    out_shape = [jax.ShapeDtypeStruct((bsz, seq, n_out), BF16)]
    n_t = 0
    if w_t is not None:
        n_t = w_t.shape[0] // BLK
        in_specs.append(pl.BlockSpec(w_t.shape, lambda b, s: (0, 0)))
        args.append(w_t)
        out_specs.append(pl.BlockSpec((None, w_t.shape[0], tm), lambda b, s: (b, 0, s)))
        out_shape.append(jax.ShapeDtypeStruct((bsz, w_t.shape[0], seq), BF16))
    if rope_chunks:
        in_specs += [pl.BlockSpec((tm, LANES), lambda b, s: (s, 0))] * 3
        args += list(rope_tabs)
    if t_rope_tabs is not None:
        in_specs += [pl.BlockSpec((HEAD_DIM, tm), lambda b, s: (0, s))] * 2
        args += list(t_rope_tabs)
    return pl.pallas_call(
        functools.partial(_inproj_kernel, n_out=n_out, q_chunks=q_chunks, rope_chunks=tuple(rope_chunks),
                          n_t=n_t, t_rope=t_rope_tabs is not None),
        grid=(bsz, seq // tm),
        in_specs=in_specs,
        out_specs=out_specs,
        out_shape=out_shape,
        compiler_params=_params(2),
        name="inproj",
    )(*args)


def _stack_masked(q_blocks, slots):
    return jnp.concatenate(
        [jnp.where(_slot_mask(s), q, jnp.zeros_like(q)) for q, s in zip(q_blocks, slots)], axis=0)


def _b_scores(qs, k):
    return lax.dot_general(qs, k, NT, preferred_element_type=F32)


def _b_softmax(s, tq, biases):
    ps, invs = [], []
    for a in range(SLOTS):
        sa = s[a * tq:(a + 1) * tq]
        if biases[a] is not None:
            nb = biases[a].shape[1]
            sa = jnp.concatenate([sa[:, :nb] + biases[a], sa[:, nb:]], axis=1)
        m = jnp.max(sa, axis=-1, keepdims=True)
        p = jnp.exp2(sa - m)
        invs.append(1.0 / jnp.sum(p, axis=-1, keepdims=True))
        ps.append(p.astype(BF16))
    return jnp.concatenate(ps, axis=0), invs


def _b_values(p, v, invs, tq):
    o = jnp.dot(p, v, preferred_element_type=F32)
    ob = o[:tq] * invs[0]
    for a in range(1, SLOTS):
        ob = jnp.where(_slot_mask(a), o[a * tq:(a + 1) * tq] * invs[a], ob)
    return ob


def _attend(qs, k, v, tq, biases):
    p, invs = _b_softmax(_b_scores(qs, k), tq, biases)
    return _b_values(p, v, invs, tq)


def _a_scores(qt, k, t, tq, window):
    nq = A_GROUPS * tq
    x = jnp.concatenate([qt[j * BLK + t * HEAD_DIM:j * BLK + (t + 1) * HEAD_DIM, :] for j in range(A_GROUPS)],
                        axis=1)
    pieces = []
    if t > 0:
        pieces.append(jnp.zeros((t * HEAD_DIM, nq), BF16))
    pieces.append(x)
    if t < SLOTS - 1:
        pieces.append(jnp.zeros(((SLOTS - 1 - t) * HEAD_DIM, nq), BF16))
    st = jnp.dot(k, jnp.concatenate(pieces, axis=0), preferred_element_type=F32)
    if window is None:
        return [st]
    return [st[:tq] + window[0], st[tq:2 * tq], st[2 * tq:3 * tq] + window[1], st[3 * tq:]]


def _a_softmax(parts, sinks_t, tq):
    nq = A_GROUPS * tq
    group = lax.broadcasted_iota(jnp.int32, (1, nq), 1) // tq
    sink = jnp.full((1, nq), sinks_t[A_GROUPS - 1], F32)
    for j in range(A_GROUPS - 1):
        sink = jnp.where(group == j, sinks_t[j], sink)
    m = sink
    for p in parts:
        m = jnp.maximum(m, jnp.max(p, axis=0, keepdims=True))
    pt = jnp.concatenate([jnp.exp2(p - m).astype(BF16) for p in parts], axis=0)
    return pt, jnp.exp2(sink - m)


def _with_ones(vt, t):
    ones = jnp.ones((ONES_ROWS, vt.shape[1]), BF16)
    return jnp.concatenate([vt[t * HEAD_DIM:(t + 1) * HEAD_DIM, :], ones], axis=0)


def _a_values(vt_ones, pt, sink_term):
    ot = jnp.dot(vt_ones, pt, preferred_element_type=F32)
    l = ot[HEAD_DIM:HEAD_DIM + 1] + sink_term
    return ot[:HEAD_DIM] * (1.0 / l)


def _a_assemble(ots, tq):
    blocks = []
    for j in range(A_GROUPS):
        otj = jnp.concatenate([ots[t][:, j * tq:(j + 1) * tq] for t in range(A_KV_HEADS)], axis=0)
        blocks.append(otj.T)
    return jnp.concatenate(blocks, axis=1)


def _attend_t(qt, k, vt, tq, window, sinks):
    heads = range(A_KV_HEADS)
    sc = [_a_scores(qt, k, t, tq, window) for t in heads]
    ots = [_a_values(_with_ones(vt, t), *_a_softmax(sc[t], sinks[t], tq)) for t in heads]
    return _a_assemble(ots, tq)


def _gate_out_norm(o, g, h, gate, wout_ref, lng, lnb):
    gated = (o * _silu(g.astype(F32))).astype(BF16)
    y = jnp.dot(gated, wout_ref[...], preferred_element_type=F32)
    r = DEEPNORM_ALPHA * h + gate * y
    mu = jnp.mean(r, axis=-1, keepdims=True)
    rc = r - mu
    var = jnp.mean(rc * rc, axis=-1, keepdims=True)
    return rc * lax.rsqrt(var + LN_EPS) * lng + lnb


def _read_sinks(sink_ref):
    return [[sink_ref[t * A_GROUPS + j] * LOG2E for j in range(A_GROUPS)] for t in range(A_KV_HEADS)]


def _attn_a_kernel(sink_ref, qt_ref, g_ref, kp_ref, kc_ref, kn_ref, vp_ref, vc_ref, vn_ref,
                   kx_ref, vx_ref, h_ref, gate_ref, wout_ref, lng_ref, lnb_ref, o_ref):
    tq = A_TQ
    i = pl.program_id(1)
    last = pl.num_programs(1) * A_SUBS - 1
    k_lat = jnp.concatenate([kp_ref[...], kc_ref[...], kn_ref[...]], axis=0)
    vt_lat = jnp.concatenate([vp_ref[...], vc_ref[...], vn_ref[...]], axis=1)
    kx, vx = kx_ref[...], vx_ref[...]
    sinks = _read_sinks(sink_ref)
    key = lax.broadcasted_iota(jnp.int32, (tq, tq), 0)
    qry = lax.broadcasted_iota(jnp.int32, (tq, tq), 1)
    heads = range(A_KV_HEADS)
    subs = range(A_SUBS)
    scores = {}
    for sub in subs:
        gi = i * A_SUBS + sub
        bias_prev = jnp.where((key >= qry) & (gi > 0), 0.0, NEG_INF).astype(F32)
        bias_next = jnp.where((key <= qry) & (gi < last), 0.0, NEG_INF).astype(F32)
        window = (jnp.concatenate([bias_prev] * A_GROUPS, axis=1),
                  jnp.concatenate([bias_next] * A_GROUPS, axis=1))
        k = jnp.concatenate([k_lat[sub * tq:(sub + 3) * tq], kx], axis=0)
        qt = qt_ref[:, sub * tq:(sub + 1) * tq]
        for t in heads:
            scores[sub, t] = _a_scores(qt, k, t, tq, window)
    outs = []
    for sub in subs:
        vt = jnp.concatenate([vt_lat[:, sub * tq:(sub + 3) * tq], vx], axis=1)
        ots = [_a_values(_with_ones(vt, t), *_a_softmax(scores[sub, t], sinks[t], tq)) for t in heads]
        outs.append(_a_assemble(ots, tq))
    o_ref[...] = _gate_out_norm(jnp.concatenate(outs, axis=0), g_ref[...], h_ref[...], gate_ref[...],
                                wout_ref, lng_ref[...], lnb_ref[...])


def _attn_a(p, pt, pc, ptc, h, mod, layer, wout, sink, lng, lnb):
    bsz, seq, _ = h.shape
    n_ctx = pc.shape[1]
    tq = A_TQ
    step = A_SUBS * tq
    nb = seq // step
    nsub = seq // tq
    kcol = D_MODEL // A_KV_DIM
    vrow = D_MODEL // A_KV_DIM
    prev_blk = lambda i: jnp.maximum(i * A_SUBS - 1, 0)
    next_blk = lambda i: jnp.minimum((i + 1) * A_SUBS, nsub - 1)
    in_specs = [
        pl.BlockSpec(memory_space=pltpu.SMEM),
        pl.BlockSpec((None, D_MODEL, step), lambda b, i: (b, 0, i)),
        pl.BlockSpec((None, step, D_MODEL), lambda b, i: (b, i, 0)),
        pl.BlockSpec((None, tq, A_KV_DIM), lambda b, i: (b, prev_blk(i), kcol)),
        pl.BlockSpec((None, step, A_KV_DIM), lambda b, i: (b, i, kcol)),
        pl.BlockSpec((None, tq, A_KV_DIM), lambda b, i: (b, next_blk(i), kcol)),
        pl.BlockSpec((None, A_KV_DIM, tq), lambda b, i: (b, vrow, prev_blk(i))),
        pl.BlockSpec((None, A_KV_DIM, step), lambda b, i: (b, vrow, i)),
        pl.BlockSpec((None, A_KV_DIM, tq), lambda b, i: (b, vrow, next_blk(i))),
        pl.BlockSpec((None, n_ctx, A_KV_DIM), lambda b, i: (b, 0, kcol)),
        pl.BlockSpec((None, A_KV_DIM, n_ctx), lambda b, i: (b, vrow, 0)),
        pl.BlockSpec((None, step, D_MODEL), lambda b, i: (b, i, 0)),
        pl.BlockSpec((None, None, None, 1, D_MODEL), lambda b, i: (layer, 2, b, 0, 0)),
        pl.BlockSpec((D_MODEL, D_MODEL), lambda b, i: (0, 0)),
        pl.BlockSpec((None, 1, D_MODEL), lambda b, i: (layer, 0, 0)),
        pl.BlockSpec((None, 1, D_MODEL), lambda b, i: (layer, 0, 0)),
    ]
    return pl.pallas_call(
        _attn_a_kernel,
        grid=(bsz, nb),
        in_specs=in_specs,
        out_specs=pl.BlockSpec((None, step, D_MODEL), lambda b, i: (b, i, 0)),
        out_shape=jax.ShapeDtypeStruct((bsz, seq, D_MODEL), F32),
        compiler_params=_params(2),
        name="attn_a",
    )(sink, pt, p, p, p, p, pt, pt, pt, pc, ptc, h, mod, wout, lng, lnb)


def _attn_b_kernel(q_ref, k_ref, v_ref, g_ref, kx_ref, vx_ref, bias_ref, h_ref, gate_ref,
                   wout_ref, lng_ref, lnb_ref, o_ref, s_ref, p_ref, *, grid_rows):
    band = B_KH * GRID_W
    tq = GRID_W
    outs = []
    for rr in range(B_ROWS):
        r = pl.program_id(1) * B_ROWS + rr
        first = jnp.clip(r - B_KH // 2, 0, grid_rows - B_KH)
        start = pl.multiple_of(first * GRID_W, GRID_W)
        dr0 = first - r + (B_KH - 1)
        rows = slice(rr * GRID_W, (rr + 1) * GRID_W)
        for blk in range(N_BLKS):
            cols = slice(blk * BLK, (blk + 1) * BLK)
            qs = _stack_masked([q_ref[rows, cols]] * SLOTS, list(range(SLOTS)))
            k = jnp.concatenate([k_ref[pl.ds(start, band), cols], kx_ref[:, cols]], axis=0)
            s_ref[blk] = _b_scores(qs, k)
        blocks = []
        for blk in range(N_BLKS):
            cols = slice(blk * BLK, (blk + 1) * BLK)
            invs = []
            for a in range(SLOTS):
                grp = slice(a * tq, (a + 1) * tq)
                bias = jnp.concatenate([bias_ref[blk * SLOTS + a, dr0 + 2 * pr] for pr in range(N_PAIR)], axis=1)
                s_lat = s_ref[blk, grp, :band] + bias
                s_ctx = s_ref[blk, grp, band:]
                m = jnp.maximum(jnp.max(s_lat, axis=-1, keepdims=True), jnp.max(s_ctx, axis=-1, keepdims=True))
                e_lat = jnp.exp2(s_lat - m)
                e_ctx = jnp.exp2(s_ctx - m)
                invs.append(1.0 / (jnp.sum(e_lat, axis=-1, keepdims=True) + jnp.sum(e_ctx, axis=-1, keepdims=True)))
                p_ref[blk, grp, :band] = e_lat.astype(BF16)
                p_ref[blk, grp, band:] = e_ctx.astype(BF16)
            v = jnp.concatenate([v_ref[pl.ds(start, band), cols], vx_ref[:, cols]], axis=0)
            blocks.append(_b_values(p_ref[blk], v, invs, tq))
        outs.append(jnp.concatenate(blocks, axis=1))
    o_ref[...] = _gate_out_norm(jnp.concatenate(outs, axis=0), g_ref[...], h_ref[...], gate_ref[...],
                                wout_ref, lng_ref[...], lnb_ref[...])


def _attn_b(p, pc, ctx_kcol, h, mod, layer, wout, bias_tab, lng, lnb):
    bsz, seq, _ = h.shape
    n_ctx = pc.shape[1]
    grid_rows = seq // GRID_W
    tq = B_ROWS * GRID_W
    scratch = (N_BLKS, SLOTS * GRID_W, B_KH * GRID_W + n_ctx)
    in_specs = [
        pl.BlockSpec((None, tq, D_MODEL), lambda b, r: (b, r, 0)),
        pl.BlockSpec((None, seq, D_MODEL), lambda b, r: (b, 0, 1)),
        pl.BlockSpec((None, seq, D_MODEL), lambda b, r: (b, 0, 2)),
        pl.BlockSpec((None, tq, D_MODEL), lambda b, r: (b, r, 3)),
        pl.BlockSpec((None, n_ctx, D_MODEL), lambda b, r: (b, 0, ctx_kcol)),
        pl.BlockSpec((None, n_ctx, D_MODEL), lambda b, r: (b, 0, ctx_kcol + 1)),
        pl.BlockSpec(bias_tab.shape, lambda b, r: (0, 0, 0, 0)),
        pl.BlockSpec((None, tq, D_MODEL), lambda b, r: (b, r, 0)),
        pl.BlockSpec((None, None, None, 1, D_MODEL), lambda b, r: (layer, 2, b, 0, 0)),
        pl.BlockSpec((D_MODEL, D_MODEL), lambda b, r: (0, 0)),
        pl.BlockSpec((None, 1, D_MODEL), lambda b, r: (layer, 0, 0)),
        pl.BlockSpec((None, 1, D_MODEL), lambda b, r: (layer, 0, 0)),
    ]
    return pl.pallas_call(
        functools.partial(_attn_b_kernel, grid_rows=grid_rows),
        grid=(bsz, grid_rows // B_ROWS),
        in_specs=in_specs,
        out_specs=pl.BlockSpec((None, tq, D_MODEL), lambda b, r: (b, r, 0)),
        out_shape=jax.ShapeDtypeStruct((bsz, seq, D_MODEL), F32),
        scratch_shapes=[pltpu.VMEM(scratch, F32), pltpu.VMEM(scratch, BF16)],
        compiler_params=_params(2),
        name="attn_b",
    )(p, p, p, p, pc, pc, bias_tab, h, mod, wout, lng, lnb)


def _ctx_attn_kernel(*refs, mixer_a):
    if mixer_a:
        sink_ref, q_ref, g_ref, k_ref, v_ref, h_ref, gate_ref, wout_ref, lng_ref, lnb_ref, o_ref = refs
        o = _attend_t(q_ref[...], k_ref[...], v_ref[...], q_ref.shape[1], None, _read_sinks(sink_ref))
    else:
        q_ref, g_ref, k_ref, v_ref, h_ref, gate_ref, wout_ref, lng_ref, lnb_ref, o_ref = refs
        n_ctx = q_ref.shape[0]
        q = q_ref[...]
        blocks = []
        for blk in range(N_BLKS):
            cols = slice(blk * BLK, (blk + 1) * BLK)
            qs = _stack_masked([q[:, cols]] * SLOTS, list(range(SLOTS)))
            blocks.append(_attend(qs, k_ref[:, cols], v_ref[:, cols], n_ctx, [None] * SLOTS))
        o = jnp.concatenate(blocks, axis=1)
    o_ref[...] = _gate_out_norm(o, g_ref[...], h_ref[...], gate_ref[...], wout_ref,
                                lng_ref[...], lnb_ref[...])


def _ctx_attn(pc, ptc, hc, mod, layer, wout, lng, lnb, *, mod_row, mixer_a, sink=None):
    bsz, n_ctx, _ = hc.shape
    if mixer_a:
        kcol = D_MODEL // A_KV_DIM
        qgkv = [
            pl.BlockSpec((None, D_MODEL, n_ctx), lambda b: (b, 0, 0)),
            pl.BlockSpec((None, n_ctx, D_MODEL), lambda b: (b, 0, 0)),
            pl.BlockSpec((None, n_ctx, A_KV_DIM), lambda b: (b, 0, kcol)),
            pl.BlockSpec((None, A_KV_DIM, n_ctx), lambda b: (b, kcol, 0)),
        ]
        args = [ptc, pc, pc, ptc]
    else:
        qgkv = [
            pl.BlockSpec((None, n_ctx, D_MODEL), lambda b: (b, 0, 0)),
            pl.BlockSpec((None, n_ctx, D_MODEL), lambda b: (b, 0, 3)),
            pl.BlockSpec((None, n_ctx, D_MODEL), lambda b: (b, 0, 1)),
            pl.BlockSpec((None, n_ctx, D_MODEL), lambda b: (b, 0, 2)),
        ]
        args = [pc, pc, pc, pc]
    in_specs = qgkv + [
        pl.BlockSpec((None, n_ctx, D_MODEL), lambda b: (b, 0, 0)),
        pl.BlockSpec((None, None, None, 1, D_MODEL), lambda b: (layer, 2, mod_row, 0, 0)),
        pl.BlockSpec((D_MODEL, D_MODEL), lambda b: (0, 0)),
        pl.BlockSpec((None, 1, D_MODEL), lambda b: (layer, 0, 0)),
        pl.BlockSpec((None, 1, D_MODEL), lambda b: (layer, 0, 0)),
    ]
    args += [hc, mod, wout, lng, lnb]
    if mixer_a:
        in_specs = [pl.BlockSpec(memory_space=pltpu.SMEM)] + in_specs
        args = [sink] + args
    return pl.pallas_call(
        functools.partial(_ctx_attn_kernel, mixer_a=mixer_a),
        grid=(bsz,),
        in_specs=in_specs,
        out_specs=pl.BlockSpec((None, n_ctx, D_MODEL), lambda b: (b, 0, 0)),
        out_shape=jax.ShapeDtypeStruct((bsz, n_ctx, D_MODEL), F32),
        compiler_params=_params(1),
        name="ctx_attn",
    )(*args)


def _regroup_heads(w, axis):
    shape = w.shape
    w = w.reshape(shape[:axis] + (A_KV_HEADS, A_GROUPS, HEAD_DIM) + shape[axis + 1:])
    return jnp.swapaxes(w, axis, axis + 1).reshape(shape)


def _prep_a(w_in, w_out):
    q = _regroup_heads(w_in[:, :D_MODEL], 1) * Q_SCALE
    k = w_in[:, D_MODEL:D_MODEL + A_KV_DIM]
    v = w_in[:, D_MODEL + A_KV_DIM:D_MODEL + 2 * A_KV_DIM]
    g = _regroup_heads(w_in[:, D_MODEL + 2 * A_KV_DIM:], 1)
    return (jnp.concatenate([g, k], axis=1).astype(BF16), jnp.concatenate([q, v], axis=1).T.astype(BF16),
            _regroup_heads(w_out, 0).astype(BF16))


def _prep_b(w_in, w_out):
    w = jnp.concatenate([w_in[:, :D_MODEL] * Q_SCALE, w_in[:, D_MODEL:]], axis=1)
    return w.astype(BF16), w_out.astype(BF16)


def _rope_tables(seq):
    t = jnp.arange(seq)
    row = (t // GRID_W).astype(F32)
    col = (t % GRID_W).astype(F32)
    n_freq = HEAD_DIM // 4
    inv = ROPE_BASE ** (-jnp.arange(n_freq, dtype=F32) / n_freq)
    ang_r = row[:, None] * inv[None]
    ang_c = col[:, None] * inv[None]
    zero = jnp.zeros_like(ang_r)
    cos = jnp.concatenate([jnp.cos(ang_r)] * 2 + [jnp.cos(ang_c)] * 2, axis=1)
    sina = jnp.concatenate([-jnp.sin(ang_r), zero, -jnp.sin(ang_c), zero], axis=1)
    sinb = jnp.concatenate([zero, jnp.sin(ang_r), zero, jnp.sin(ang_c)], axis=1)
    rep = LANES // HEAD_DIM
    return tuple(jnp.tile(x, (1, rep)) for x in (cos, sina, sinb)), (cos.T, (sina + sinb).T)


def _bias_tab_kernel(rb_ref, o_ref):
    shape = (GRID_W, LANES)
    qc = lax.broadcasted_iota(jnp.int32, shape, 0)
    lane = lax.broadcasted_iota(jnp.int32, shape, 1)
    kc = lane % GRID_W
    cs = jnp.clip(qc - B_KW // 2, 0, GRID_W - B_KW)
    valid = (kc >= cs) & (kc < cs + B_KW)
    left = lane < GRID_W
    sh_left = LANES - (B_KW - 1)
    sh_right = (sh_left + GRID_W) % LANES
    rolled = []
    for dr in range(N_DR):
        row = jnp.broadcast_to(rb_ref[dr:dr + 1, :], shape)
        rolled.append((pltpu.roll(row, sh_left, 1, stride=1, stride_axis=0),
                       pltpu.roll(row, sh_right, 1, stride=1, stride_axis=0)))
    for d in range(N_DR - 1):
        tile = jnp.where(left, rolled[d][0], rolled[d + 1][1])
        o_ref[d] = jnp.where(valid, tile * LOG2E, NEG_INF)


def _b_bias_table(rel_bias):
    n_dc = rel_bias.shape[2]
    rb = jnp.pad(rel_bias.astype(F32), ((0, 0), (0, 0), (0, LANES - n_dc)))
    return pl.pallas_call(
        _bias_tab_kernel,
        grid=(N_HEADS,),
        in_specs=[pl.BlockSpec((None, N_DR, LANES), lambda hd: (hd, 0, 0))],
        out_specs=pl.BlockSpec((None, N_DR - 1, GRID_W, LANES), lambda hd: (hd, 0, 0, 0)),
        out_shape=jax.ShapeDtypeStruct((N_HEADS, N_DR - 1, GRID_W, LANES), F32),
        compiler_params=_params(1),
        name="bias_tab",
    )(rb)


def kernel(x, c, ctx, c_ctx, w_ada, b_ada, ln_g, ln_b, a_w_in, a_w_out, a_sink, b_w_in, b_w_out, b_rel_bias):
    bsz, seq, _ = x.shape
    n_ctx = ctx.shape[1]
    assert bsz + 1 <= MOD_ROWS
    cc = jnp.concatenate([c, c_ctx[None, :], jnp.zeros((MOD_ROWS - bsz - 1, D_MODEL), F32)], axis=0)
    mod = _ada(cc, w_ada, b_ada)
    lng = ln_g.reshape(DEPTH, 1, D_MODEL)
    lnb = ln_b.reshape(DEPTH, 1, D_MODEL)
    rope, rope_t = _rope_tables(seq)
    h, hc = x, ctx
    for i in range(DEPTH):
        j = i // 2
        ctx_out = i < DEPTH - 1
        if i % 2 == 0:
            w_in, w_t, w_out = _prep_a(a_w_in[j], a_w_out[j])
            p, pt = _inproj(h, mod, i, w_in, tm=PROJ_ROWS, mod_row=None, q_chunks=0, w_t=w_t,
                            rope_tabs=rope, rope_chunks=(N_BLKS,), t_rope_tabs=rope_t)
            pc, ptc = _inproj(hc, mod, i, w_in, tm=n_ctx, mod_row=bsz, q_chunks=0, w_t=w_t)
            h = _attn_a(p, pt, pc, ptc, h, mod, i, w_out, a_sink[j], lng, lnb)
            if ctx_out:
                hc = _ctx_attn(pc, ptc, hc, mod, i, w_out, lng, lnb, mod_row=bsz, mixer_a=True,
                               sink=a_sink[j])
        else:
            w_in, w_out = _prep_b(b_w_in[j], b_w_out[j])
            (p,) = _inproj(h, mod, i, w_in, tm=PROJ_ROWS, mod_row=None)
            if ctx_out:
                (pc,) = _inproj(hc, mod, i, w_in, tm=n_ctx, mod_row=bsz)
                ctx_kcol = 1
            else:
                (pc,) = _inproj(hc, mod, i, w_in[:, D_MODEL:3 * D_MODEL], tm=n_ctx, mod_row=bsz, q_chunks=0)
                ctx_kcol = 0
            h = _attn_b(p, pc, ctx_kcol, h, mod, i, w_out, _b_bias_table(b_rel_bias[j]), lng, lnb)
            if ctx_out:
                hc = _ctx_attn(pc, None, hc, mod, i, w_out, lng, lnb, mod_row=bsz, mixer_a=False)
    return h
```

```python
import functools
import math

import jax
import jax.numpy as jnp
from jax import lax
from jax.experimental import pallas as pl
from jax.experimental.pallas import tpu as pltpu

D_MODEL = 1024
DEPTH = 4
GRID_W = 64
HEAD_DIM = 64
N_HEADS = 16
A_KV_HEADS = 4
A_GROUPS = 4
A_WINDOW = 128
A_KV_DIM = A_KV_HEADS * HEAD_DIM
B_KH = 8
B_KW = 16
ROPE_BASE = 10000.0
LN_EPS = 1e-5
NEG_INF = -1e30
DEEPNORM_ALPHA = (2.0 * DEPTH) ** 0.25
Q_SCALE = HEAD_DIM ** -0.5
LOG2E = math.log2(math.e)

LANES = 128
BLK = 256
SLOTS = BLK // HEAD_DIM
N_BLKS = D_MODEL // BLK
MOD_ROWS = 40
VMEM_LIMIT = 56 * 1024 * 1024
PROJ_ROWS = 1024
A_TQ = 128
A_SUBS = 4
ONES_ROWS = 16
B_ROWS = 4
N_DR = 2 * B_KH - 1
N_PAIR = B_KH // 2

assert A_TQ == A_WINDOW

F32 = jnp.float32
BF16 = jnp.bfloat16
NT = (((1,), (1,)), ((), ()))


def _silu(x):
    return x / (1.0 + jnp.exp(-x))


def _slot_mask(slot):
    lane = lax.broadcasted_iota(jnp.int32, (1, BLK), 1)
    return (lane // HEAD_DIM) == slot


def _params(n_axes):
    return pltpu.CompilerParams(
        dimension_semantics=("arbitrary",) * n_axes, vmem_limit_bytes=VMEM_LIMIT)


def _ada_kernel(c_ref, w_ref, b_ref, o_ref):
    sc = _silu(c_ref[...])
    o_ref[...] = jnp.dot(sc, w_ref[...], preferred_element_type=F32,
                         precision=lax.Precision.HIGHEST) + b_ref[...]


def _ada(cc, w_ada, b_ada):
    out = pl.pallas_call(
        _ada_kernel,
        grid=(DEPTH, 3),
        in_specs=[
            pl.BlockSpec((MOD_ROWS, D_MODEL), lambda d, j: (0, 0)),
            pl.BlockSpec((None, D_MODEL, D_MODEL), lambda d, j: (d, 0, j)),
            pl.BlockSpec((None, None, 1, D_MODEL), lambda d, j: (d, j, 0, 0)),
        ],
        out_specs=pl.BlockSpec((None, None, MOD_ROWS, D_MODEL), lambda d, j: (d, j, 0, 0)),
        out_shape=jax.ShapeDtypeStruct((DEPTH, 3, MOD_ROWS, D_MODEL), F32),
        compiler_params=_params(2),
        name="ada_mod",
    )(cc, w_ada, b_ada.reshape(DEPTH, 3, 1, D_MODEL))
    return out.reshape(DEPTH, 3, MOD_ROWS, 1, D_MODEL)


def _inproj_kernel(*refs, n_out, q_chunks, rope_chunks, n_t, t_rope):
    refs = list(refs)
    x_ref, shift_ref, scale_ref, w_ref = refs[:4]
    refs = refs[4:]
    if n_t:
        wt_ref = refs.pop(0)
    if rope_chunks:
        cos_ref, sina_ref, sinb_ref = refs[:3]
        refs = refs[3:]
    if t_rope:
        cost_ref, sint_ref = refs[:2]
        refs = refs[2:]
    o_ref = refs[0]
    u = (x_ref[...] * (1.0 + scale_ref[...]) + shift_ref[...]).astype(BF16)
    for ci in range(n_out // BLK):
        c0 = ci * BLK
        acc = jnp.dot(u, w_ref[:, c0:c0 + BLK], preferred_element_type=F32)
        if ci < q_chunks:
            acc = acc * LOG2E
        if ci in rope_chunks:
            cos, sina, sinb = cos_ref[...], sina_ref[...], sinb_ref[...]
            parts = []
            for l0 in range(0, BLK, LANES):
                a = acc[:, l0:l0 + LANES]
                parts.append(a * cos + pltpu.roll(a, LANES - 16, 1) * sina + pltpu.roll(a, 16, 1) * sinb)
            acc = jnp.concatenate(parts, axis=1)
        o_ref[:, c0:c0 + BLK] = acc.astype(BF16)
    if n_t:
        t_ref = refs[1]
        pair = HEAD_DIM // 4
        for ci in range(n_t):
            r0 = ci * BLK
            acc = lax.dot_general(wt_ref[r0:r0 + BLK, :], u, NT, preferred_element_type=F32)
            if ci < N_BLKS:
                acc = acc * LOG2E
                if t_rope:
                    cost, sint = cost_ref[...], sint_ref[...]
                    heads = []
                    for r in range(0, BLK, HEAD_DIM):
                        xh = acc[r:r + HEAD_DIM]
                        swapped = jnp.concatenate([xh[pair:2 * pair], xh[:pair], xh[3 * pair:], xh[2 * pair:3 * pair]], axis=0)
                        heads.append(xh * cost + swapped * sint)
                    acc = jnp.concatenate(heads, axis=0)
            t_ref[r0:r0 + BLK, :] = acc.astype(BF16)


def _inproj(h, mod, layer, w, *, tm, mod_row, q_chunks=N_BLKS, w_t=None, rope_tabs=None, rope_chunks=(),
            t_rope_tabs=None):
    bsz, seq, _ = h.shape
    n_out = w.shape[1]
    row = (lambda b: b) if mod_row is None else (lambda b: mod_row)
    in_specs = [
        pl.BlockSpec((None, tm, D_MODEL), lambda b, s: (b, s, 0)),
        pl.BlockSpec((None, None, None, 1, D_MODEL), lambda b, s: (layer, 0, row(b), 0, 0)),
        pl.BlockSpec((None, None, None, 1, D_MODEL), lambda b, s: (layer, 1, row(b), 0, 0)),
        pl.BlockSpec((D_MODEL, n_out), lambda b, s: (0, 0)),
    ]
    args = [h, mod, mod, w]
    out_specs = [pl.BlockSpec((None, tm, n_out), lambda b, s: (b, s, 0))]
    out_shape = [jax.ShapeDtypeStruct((bsz, seq, n_out), BF16)]
    n_t = 0
    if w_t is not None:
        n_t = w_t.shape[0] // BLK
        in_specs.append(pl.BlockSpec(w_t.shape, lambda b, s: (0, 0)))
        args.append(w_t)
        out_specs.append(pl.BlockSpec((None, w_t.shape[0], tm), lambda b, s: (b, 0, s)))
        out_shape.append(jax.ShapeDtypeStruct((bsz, w_t.shape[0], seq), BF16))
    if rope_chunks:
        in_specs += [pl.BlockSpec((tm, LANES), lambda b, s: (s, 0))] * 3
        args += list(rope_tabs)
    if t_rope_tabs is not None:
        in_specs += [pl.BlockSpec((HEAD_DIM, tm), lambda b, s: (0, s))] * 2
        args += list(t_rope_tabs)
    return pl.pallas_call(
        functools.partial(_inproj_kernel, n_out=n_out, q_chunks=q_chunks, rope_chunks=tuple(rope_chunks),
                          n_t=n_t, t_rope=t_rope_tabs is not None),
        grid=(bsz, seq // tm),
        in_specs=in_specs,
        out_specs=out_specs,
        out_shape=out_shape,
        compiler_params=_params(2),
        name="inproj",
    )(*args)


def _stack_masked(q_blocks, slots):
    return jnp.concatenate(
        [jnp.where(_slot_mask(s), q, jnp.zeros_like(q)) for q, s in zip(q_blocks, slots)], axis=0)


def _b_scores(qs, k):
    return lax.dot_general(qs, k, NT, preferred_element_type=F32)


def _b_softmax(s, tq, biases):
    ps, invs = [], []
    for a in range(SLOTS):
        sa = s[a * tq:(a + 1) * tq]
        if biases[a] is not None:
            nb = biases[a].shape[1]
            sa = jnp.concatenate([sa[:, :nb] + biases[a], sa[:, nb:]], axis=1)
        m = jnp.max(sa, axis=-1, keepdims=True)
        p = jnp.exp2(sa - m)
        invs.append(1.0 / jnp.sum(p, axis=-1, keepdims=True))
        ps.append(p.astype(BF16))
    return jnp.concatenate(ps, axis=0), invs


def _b_values(p, v, invs, tq):
    o = jnp.dot(p, v, preferred_element_type=F32)
    ob = o[:tq] * invs[0]
    for a in range(1, SLOTS):
        ob = jnp.where(_slot_mask(a), o[a * tq:(a + 1) * tq] * invs[a], ob)
    return ob


def _attend(qs, k, v, tq, biases):
    p, invs = _b_softmax(_b_scores(qs, k), tq, biases)
    return _b_values(p, v, invs, tq)


def _a_scores(qt, k, t, tq, window):
    nq = A_GROUPS * tq
    x = jnp.concatenate([qt[j * BLK + t * HEAD_DIM:j * BLK + (t + 1) * HEAD_DIM, :] for j in range(A_GROUPS)],
                        axis=1)
    pieces = []
    if t > 0:
        pieces.append(jnp.zeros((t * HEAD_DIM, nq), BF16))
    pieces.append(x)
    if t < SLOTS - 1:
        pieces.append(jnp.zeros(((SLOTS - 1 - t) * HEAD_DIM, nq), BF16))
    st = jnp.dot(k, jnp.concatenate(pieces, axis=0), preferred_element_type=F32)
    if window is None:
        return [st]
    return [st[:tq] + window[0], st[tq:2 * tq], st[2 * tq:3 * tq] + window[1], st[3 * tq:]]


def _sink_row(sinks_t, tq):
    nq = A_GROUPS * tq
    group = lax.broadcasted_iota(jnp.int32, (1, nq), 1) // tq
    sink = jnp.full((1, nq), sinks_t[A_GROUPS - 1], F32)
    for j in range(A_GROUPS - 1):
        sink = jnp.where(group == j, sinks_t[j], sink)
    return sink


def _a_softmax(parts, sinks_t, tq):
    sink = _sink_row(sinks_t, tq)
    m = sink
    for p in parts:
        m = jnp.maximum(m, jnp.max(p, axis=0, keepdims=True))
    pt = jnp.concatenate([jnp.exp2(p - m).astype(BF16) for p in parts], axis=0)
    return pt, jnp.exp2(sink - m)


def _with_ones(vt, t):
    ones = jnp.ones((ONES_ROWS, vt.shape[1]), BF16)
    return jnp.concatenate([vt[t * HEAD_DIM:(t + 1) * HEAD_DIM, :], ones], axis=0)


def _a_values(vt_ones, pt, sink_term):
    ot = jnp.dot(vt_ones, pt, preferred_element_type=F32)
    l = ot[HEAD_DIM:HEAD_DIM + 1] + sink_term
    return ot[:HEAD_DIM] * (1.0 / l)


def _a_assemble(ots, tq):
    blocks = []
    for j in range(A_GROUPS):
        otj = jnp.concatenate([ots[t][:, j * tq:(j + 1) * tq] for t in range(A_KV_HEADS)], axis=0)
        blocks.append(otj.T)
    return jnp.concatenate(blocks, axis=1)


def _attend_t(qt, k, vt, tq, window, sinks):
    heads = range(A_KV_HEADS)
    sc = [_a_scores(qt, k, t, tq, window) for t in heads]
    ots = [_a_values(_with_ones(vt, t), *_a_softmax(sc[t], sinks[t], tq)) for t in heads]
    return _a_assemble(ots, tq)


def _gate_out_norm(o, g, h, gate, wout_ref, lng, lnb):
    gated = (o * _silu(g.astype(F32))).astype(BF16)
    y = jnp.dot(gated, wout_ref[...], preferred_element_type=F32)
    r = DEEPNORM_ALPHA * h + gate * y
    mu = jnp.mean(r, axis=-1, keepdims=True)
    rc = r - mu
    var = jnp.mean(rc * rc, axis=-1, keepdims=True)
    return rc * lax.rsqrt(var + LN_EPS) * lng + lnb


def _read_sinks(sink_ref):
    return [[sink_ref[t * A_GROUPS + j] * LOG2E for j in range(A_GROUPS)] for t in range(A_KV_HEADS)]


def _attn_a_kernel(sink_ref, qt_ref, g_ref, kp_ref, kc_ref, kn_ref, vp_ref, vc_ref, vn_ref,
                   kx_ref, vx_ref, h_ref, gate_ref, wout_ref, lng_ref, lnb_ref, o_ref, s_ref, p_ref):
    tq = A_TQ
    i = pl.program_id(1)
    last = pl.num_programs(1) * A_SUBS - 1
    k_lat = jnp.concatenate([kp_ref[...], kc_ref[...], kn_ref[...]], axis=0)
    vt_lat = jnp.concatenate([vp_ref[...], vc_ref[...], vn_ref[...]], axis=1)
    kx, vx = kx_ref[...], vx_ref[...]
    nk = s_ref.shape[1]
    sinks = _read_sinks(sink_ref)
    key = lax.broadcasted_iota(jnp.int32, (tq, tq), 0)
    qry = lax.broadcasted_iota(jnp.int32, (tq, tq), 1)
    heads = range(A_KV_HEADS)
    subs = range(A_SUBS)
    for sub in subs:
        gi = i * A_SUBS + sub
        bias_prev = jnp.where((key >= qry) & (gi > 0), 0.0, NEG_INF).astype(F32)
        bias_next = jnp.where((key <= qry) & (gi < last), 0.0, NEG_INF).astype(F32)
        window = (jnp.concatenate([bias_prev] * A_GROUPS, axis=1),
                  jnp.concatenate([bias_next] * A_GROUPS, axis=1))
        k = jnp.concatenate([k_lat[sub * tq:(sub + 3) * tq], kx], axis=0)
        qt = qt_ref[:, sub * tq:(sub + 1) * tq]
        for t in heads:
            c = sub * A_KV_HEADS + t
            r0 = 0
            for part in _a_scores(qt, k, t, tq, window):
                s_ref[c, r0:r0 + part.shape[0], :] = part
                r0 += part.shape[0]
    outs = []
    for sub in subs:
        vt = jnp.concatenate([vt_lat[:, sub * tq:(sub + 3) * tq], vx], axis=1)
        ots = []
        for t in heads:
            c = sub * A_KV_HEADS + t
            sink = _sink_row(sinks[t], tq)
            m = sink
            for r0 in range(0, nk, tq):
                m = jnp.maximum(m, jnp.max(s_ref[c, r0:r0 + tq, :], axis=0, keepdims=True))
            for r0 in range(0, nk, tq):
                p_ref[c, r0:r0 + tq, :] = jnp.exp2(s_ref[c, r0:r0 + tq, :] - m).astype(BF16)
            ots.append(_a_values(_with_ones(vt, t), p_ref[c], jnp.exp2(sink - m)))
        outs.append(_a_assemble(ots, tq))
    o_ref[...] = _gate_out_norm(jnp.concatenate(outs, axis=0), g_ref[...], h_ref[...], gate_ref[...],
                                wout_ref, lng_ref[...], lnb_ref[...])


def _attn_a(p, pt, pc, ptc, h, mod, layer, wout, sink, lng, lnb):
    bsz, seq, _ = h.shape
    n_ctx = pc.shape[1]
    tq = A_TQ
    step = A_SUBS * tq
    nb = seq // step
    nsub = seq // tq
    kcol = D_MODEL // A_KV_DIM
    vrow = D_MODEL // A_KV_DIM
    prev_blk = lambda i: jnp.maximum(i * A_SUBS - 1, 0)
    next_blk = lambda i: jnp.minimum((i + 1) * A_SUBS, nsub - 1)
    in_specs = [
        pl.BlockSpec(memory_space=pltpu.SMEM),
        pl.BlockSpec((None, D_MODEL, step), lambda b, i: (b, 0, i)),
        pl.BlockSpec((None, step, D_MODEL), lambda b, i: (b, i, 0)),
        pl.BlockSpec((None, tq, A_KV_DIM), lambda b, i: (b, prev_blk(i), kcol)),
        pl.BlockSpec((None, step, A_KV_DIM), lambda b, i: (b, i, kcol)),
        pl.BlockSpec((None, tq, A_KV_DIM), lambda b, i: (b, next_blk(i), kcol)),
        pl.BlockSpec((None, A_KV_DIM, tq), lambda b, i: (b, vrow, prev_blk(i))),
        pl.BlockSpec((None, A_KV_DIM, step), lambda b, i: (b, vrow, i)),
        pl.BlockSpec((None, A_KV_DIM, tq), lambda b, i: (b, vrow, next_blk(i))),
        pl.BlockSpec((None, n_ctx, A_KV_DIM), lambda b, i: (b, 0, kcol)),
        pl.BlockSpec((None, A_KV_DIM, n_ctx), lambda b, i: (b, vrow, 0)),
        pl.BlockSpec((None, step, D_MODEL), lambda b, i: (b, i, 0)),
        pl.BlockSpec((None, None, None, 1, D_MODEL), lambda b, i: (layer, 2, b, 0, 0)),
        pl.BlockSpec((D_MODEL, D_MODEL), lambda b, i: (0, 0)),
        pl.BlockSpec((None, 1, D_MODEL), lambda b, i: (layer, 0, 0)),
        pl.BlockSpec((None, 1, D_MODEL), lambda b, i: (layer, 0, 0)),
    ]
    return pl.pallas_call(
        _attn_a_kernel,
        grid=(bsz, nb),
        in_specs=in_specs,
        out_specs=pl.BlockSpec((None, step, D_MODEL), lambda b, i: (b, i, 0)),
        out_shape=jax.ShapeDtypeStruct((bsz, seq, D_MODEL), F32),
        scratch_shapes=[pltpu.VMEM((A_SUBS * A_KV_HEADS, 3 * tq + n_ctx, A_GROUPS * tq), F32),
                        pltpu.VMEM((A_SUBS * A_KV_HEADS, 3 * tq + n_ctx, A_GROUPS * tq), BF16)],
        compiler_params=_params(2),
        name="attn_a",
    )(sink, pt, p, p, p, p, pt, pt, pt, pc, ptc, h, mod, wout, lng, lnb)


def _attn_b_kernel(q_ref, k_ref, v_ref, g_ref, kx_ref, vx_ref, bias_ref, h_ref, gate_ref,
                   wout_ref, lng_ref, lnb_ref, o_ref, s_ref, p_ref, *, grid_rows):
    band = B_KH * GRID_W
    tq = GRID_W
    outs = []
    for rr in range(B_ROWS):
        r = pl.program_id(1) * B_ROWS + rr
        first = jnp.clip(r - B_KH // 2, 0, grid_rows - B_KH)
        start = pl.multiple_of(first * GRID_W, GRID_W)
        dr0 = first - r + (B_KH - 1)
        rows = slice(rr * GRID_W, (rr + 1) * GRID_W)
        for blk in range(N_BLKS):
            cols = slice(blk * BLK, (blk + 1) * BLK)
            qs = _stack_masked([q_ref[rows, cols]] * SLOTS, list(range(SLOTS)))
            k = jnp.concatenate([k_ref[pl.ds(start, band), cols], kx_ref[:, cols]], axis=0)
            s_ref[blk] = _b_scores(qs, k)
        blocks = []
        for blk in range(N_BLKS):
            cols = slice(blk * BLK, (blk + 1) * BLK)
            invs = []
            for a in range(SLOTS):
                grp = slice(a * tq, (a + 1) * tq)
                bias = jnp.concatenate([bias_ref[blk * SLOTS + a, dr0 + 2 * pr] for pr in range(N_PAIR)], axis=1)
                s_lat = s_ref[blk, grp, :band] + bias
                s_ctx = s_ref[blk, grp, band:]
                m = jnp.maximum(jnp.max(s_lat, axis=-1, keepdims=True), jnp.max(s_ctx, axis=-1, keepdims=True))
                e_lat = jnp.exp2(s_lat - m)
                e_ctx = jnp.exp2(s_ctx - m)
                invs.append(1.0 / (jnp.sum(e_lat, axis=-1, keepdims=True) + jnp.sum(e_ctx, axis=-1, keepdims=True)))
                p_ref[blk, grp, :band] = e_lat.astype(BF16)
                p_ref[blk, grp, band:] = e_ctx.astype(BF16)
            v = jnp.concatenate([v_ref[pl.ds(start, band), cols], vx_ref[:, cols]], axis=0)
            blocks.append(_b_values(p_ref[blk], v, invs, tq))
        outs.append(jnp.concatenate(blocks, axis=1))
    o_ref[...] = _gate_out_norm(jnp.concatenate(outs, axis=0), g_ref[...], h_ref[...], gate_ref[...],
                                wout_ref, lng_ref[...], lnb_ref[...])


def _attn_b(p, pc, ctx_kcol, h, mod, layer, wout, bias_tab, lng, lnb):
    bsz, seq, _ = h.shape
    n_ctx = pc.shape[1]
    grid_rows = seq // GRID_W
    tq = B_ROWS * GRID_W
    scratch = (N_BLKS, SLOTS * GRID_W, B_KH * GRID_W + n_ctx)
    in_specs = [
        pl.BlockSpec((None, tq, D_MODEL), lambda b, r: (b, r, 0)),
        pl.BlockSpec((None, seq, D_MODEL), lambda b, r: (b, 0, 1)),
        pl.BlockSpec((None, seq, D_MODEL), lambda b, r: (b, 0, 2)),
        pl.BlockSpec((None, tq, D_MODEL), lambda b, r: (b, r, 3)),
        pl.BlockSpec((None, n_ctx, D_MODEL), lambda b, r: (b, 0, ctx_kcol)),
        pl.BlockSpec((None, n_ctx, D_MODEL), lambda b, r: (b, 0, ctx_kcol + 1)),
        pl.BlockSpec(bias_tab.shape, lambda b, r: (0, 0, 0, 0)),
        pl.BlockSpec((None, tq, D_MODEL), lambda b, r: (b, r, 0)),
        pl.BlockSpec((None, None, None, 1, D_MODEL), lambda b, r: (layer, 2, b, 0, 0)),
        pl.BlockSpec((D_MODEL, D_MODEL), lambda b, r: (0, 0)),
        pl.BlockSpec((None, 1, D_MODEL), lambda b, r: (layer, 0, 0)),
        pl.BlockSpec((None, 1, D_MODEL), lambda b, r: (layer, 0, 0)),
    ]
    return pl.pallas_call(
        functools.partial(_attn_b_kernel, grid_rows=grid_rows),
        grid=(bsz, grid_rows // B_ROWS),
        in_specs=in_specs,
        out_specs=pl.BlockSpec((None, tq, D_MODEL), lambda b, r: (b, r, 0)),
        out_shape=jax.ShapeDtypeStruct((bsz, seq, D_MODEL), F32),
        scratch_shapes=[pltpu.VMEM(scratch, F32), pltpu.VMEM(scratch, BF16)],
        compiler_params=_params(2),
        name="attn_b",
    )(p, p, p, p, pc, pc, bias_tab, h, mod, wout, lng, lnb)


def _ctx_attn_kernel(*refs, mixer_a):
    if mixer_a:
        sink_ref, q_ref, g_ref, k_ref, v_ref, h_ref, gate_ref, wout_ref, lng_ref, lnb_ref, o_ref = refs
        o = _attend_t(q_ref[...], k_ref[...], v_ref[...], q_ref.shape[1], None, _read_sinks(sink_ref))
    else:
        q_ref, g_ref, k_ref, v_ref, h_ref, gate_ref, wout_ref, lng_ref, lnb_ref, o_ref = refs
        n_ctx = q_ref.shape[0]
        q = q_ref[...]
        blocks = []
        for blk in range(N_BLKS):
            cols = slice(blk * BLK, (blk + 1) * BLK)
            qs = _stack_masked([q[:, cols]] * SLOTS, list(range(SLOTS)))
            blocks.append(_attend(qs, k_ref[:, cols], v_ref[:, cols], n_ctx, [None] * SLOTS))
        o = jnp.concatenate(blocks, axis=1)
    o_ref[...] = _gate_out_norm(o, g_ref[...], h_ref[...], gate_ref[...], wout_ref,
                                lng_ref[...], lnb_ref[...])


def _ctx_attn(pc, ptc, hc, mod, layer, wout, lng, lnb, *, mod_row, mixer_a, sink=None):
    bsz, n_ctx, _ = hc.shape
    if mixer_a:
        kcol = D_MODEL // A_KV_DIM
        qgkv = [
            pl.BlockSpec((None, D_MODEL, n_ctx), lambda b: (b, 0, 0)),
            pl.BlockSpec((None, n_ctx, D_MODEL), lambda b: (b, 0, 0)),
            pl.BlockSpec((None, n_ctx, A_KV_DIM), lambda b: (b, 0, kcol)),
            pl.BlockSpec((None, A_KV_DIM, n_ctx), lambda b: (b, kcol, 0)),
        ]
        args = [ptc, pc, pc, ptc]
    else:
        qgkv = [
            pl.BlockSpec((None, n_ctx, D_MODEL), lambda b: (b, 0, 0)),
            pl.BlockSpec((None, n_ctx, D_MODEL), lambda b: (b, 0, 3)),
            pl.BlockSpec((None, n_ctx, D_MODEL), lambda b: (b, 0, 1)),
            pl.BlockSpec((None, n_ctx, D_MODEL), lambda b: (b, 0, 2)),
        ]
        args = [pc, pc, pc, pc]
    in_specs = qgkv + [
        pl.BlockSpec((None, n_ctx, D_MODEL), lambda b: (b, 0, 0)),
        pl.BlockSpec((None, None, None, 1, D_MODEL), lambda b: (layer, 2, mod_row, 0, 0)),
        pl.BlockSpec((D_MODEL, D_MODEL), lambda b: (0, 0)),
        pl.BlockSpec((None, 1, D_MODEL), lambda b: (layer, 0, 0)),
        pl.BlockSpec((None, 1, D_MODEL), lambda b: (layer, 0, 0)),
    ]
    args += [hc, mod, wout, lng, lnb]
    if mixer_a:
        in_specs = [pl.BlockSpec(memory_space=pltpu.SMEM)] + in_specs
        args = [sink] + args
    return pl.pallas_call(
        functools.partial(_ctx_attn_kernel, mixer_a=mixer_a),
        grid=(bsz,),
        in_specs=in_specs,
        out_specs=pl.BlockSpec((None, n_ctx, D_MODEL), lambda b: (b, 0, 0)),
        out_shape=jax.ShapeDtypeStruct((bsz, n_ctx, D_MODEL), F32),
        compiler_params=_params(1),
        name="ctx_attn",
    )(*args)


def _regroup_heads(w, axis):
    shape = w.shape
    w = w.reshape(shape[:axis] + (A_KV_HEADS, A_GROUPS, HEAD_DIM) + shape[axis + 1:])
    return jnp.swapaxes(w, axis, axis + 1).reshape(shape)


def _prep_a(w_in, w_out):
    q = _regroup_heads(w_in[:, :D_MODEL], 1) * Q_SCALE
    k = w_in[:, D_MODEL:D_MODEL + A_KV_DIM]
    v = w_in[:, D_MODEL + A_KV_DIM:D_MODEL + 2 * A_KV_DIM]
    g = _regroup_heads(w_in[:, D_MODEL + 2 * A_KV_DIM:], 1)
    return (jnp.concatenate([g, k], axis=1).astype(BF16), jnp.concatenate([q, v], axis=1).T.astype(BF16),
            _regroup_heads(w_out, 0).astype(BF16))


def _prep_b(w_in, w_out):
    w = jnp.concatenate([w_in[:, :D_MODEL] * Q_SCALE, w_in[:, D_MODEL:]], axis=1)
    return w.astype(BF16), w_out.astype(BF16)


def _rope_tables(seq):
    t = jnp.arange(seq)
    row = (t // GRID_W).astype(F32)
    col = (t % GRID_W).astype(F32)
    n_freq = HEAD_DIM // 4
    inv = ROPE_BASE ** (-jnp.arange(n_freq, dtype=F32) / n_freq)
    ang_r = row[:, None] * inv[None]
    ang_c = col[:, None] * inv[None]
    zero = jnp.zeros_like(ang_r)
    cos = jnp.concatenate([jnp.cos(ang_r)] * 2 + [jnp.cos(ang_c)] * 2, axis=1)
    sina = jnp.concatenate([-jnp.sin(ang_r), zero, -jnp.sin(ang_c), zero], axis=1)
    sinb = jnp.concatenate([zero, jnp.sin(ang_r), zero, jnp.sin(ang_c)], axis=1)
    rep = LANES // HEAD_DIM
    return tuple(jnp.tile(x, (1, rep)) for x in (cos, sina, sinb)), (cos.T, (sina + sinb).T)


def _bias_tab_kernel(rb_ref, o_ref):
    shape = (GRID_W, LANES)
    qc = lax.broadcasted_iota(jnp.int32, shape, 0)
    lane = lax.broadcasted_iota(jnp.int32, shape, 1)
    kc = lane % GRID_W
    cs = jnp.clip(qc - B_KW // 2, 0, GRID_W - B_KW)
    valid = (kc >= cs) & (kc < cs + B_KW)
    left = lane < GRID_W
    sh_left = LANES - (B_KW - 1)
    sh_right = (sh_left + GRID_W) % LANES
    rolled = []
    for dr in range(N_DR):
        row = jnp.broadcast_to(rb_ref[dr:dr + 1, :], shape)
        rolled.append((pltpu.roll(row, sh_left, 1, stride=1, stride_axis=0),
                       pltpu.roll(row, sh_right, 1, stride=1, stride_axis=0)))
    for d in range(N_DR - 1):
        tile = jnp.where(left, rolled[d][0], rolled[d + 1][1])
        o_ref[d] = jnp.where(valid, tile * LOG2E, NEG_INF)


def _b_bias_table(rel_bias):
    n_dc = rel_bias.shape[2]
    rb = jnp.pad(rel_bias.astype(F32), ((0, 0), (0, 0), (0, LANES - n_dc)))
    return pl.pallas_call(
        _bias_tab_kernel,
        grid=(N_HEADS,),
        in_specs=[pl.BlockSpec((None, N_DR, LANES), lambda hd: (hd, 0, 0))],
        out_specs=pl.BlockSpec((None, N_DR - 1, GRID_W, LANES), lambda hd: (hd, 0, 0, 0)),
        out_shape=jax.ShapeDtypeStruct((N_HEADS, N_DR - 1, GRID_W, LANES), F32),
        compiler_params=_params(1),
        name="bias_tab",
    )(rb)


def kernel(x, c, ctx, c_ctx, w_ada, b_ada, ln_g, ln_b, a_w_in, a_w_out, a_sink, b_w_in, b_w_out, b_rel_bias):
    bsz, seq, _ = x.shape
    n_ctx = ctx.shape[1]
    assert bsz + 1 <= MOD_ROWS
    cc = jnp.concatenate([c, c_ctx[None, :], jnp.zeros((MOD_ROWS - bsz - 1, D_MODEL), F32)], axis=0)
    mod = _ada(cc, w_ada, b_ada)
    lng = ln_g.reshape(DEPTH, 1, D_MODEL)
    lnb = ln_b.reshape(DEPTH, 1, D_MODEL)
    rope, rope_t = _rope_tables(seq)
    h, hc = x, ctx
    for i in range(DEPTH):
        j = i // 2
        ctx_out = i < DEPTH - 1
        if i % 2 == 0:
            w_in, w_t, w_out = _prep_a(a_w_in[j], a_w_out[j])
            p, pt = _inproj(h, mod, i, w_in, tm=PROJ_ROWS, mod_row=None, q_chunks=0, w_t=w_t,
                            rope_tabs=rope, rope_chunks=(N_BLKS,), t_rope_tabs=rope_t)
            pc, ptc = _inproj(hc, mod, i, w_in, tm=n_ctx, mod_row=bsz, q_chunks=0, w_t=w_t)
            h = _attn_a(p, pt, pc, ptc, h, mod, i, w_out, a_sink[j], lng, lnb)
            if ctx_out:
                hc = _ctx_attn(pc, ptc, hc, mod, i, w_out, lng, lnb, mod_row=bsz, mixer_a=True,
                               sink=a_sink[j])
        else:
            w_in, w_out = _prep_b(b_w_in[j], b_w_out[j])
            (p,) = _inproj(h, mod, i, w_in, tm=PROJ_ROWS, mod_row=None)
            if ctx_out:
                (pc,) = _inproj(hc, mod, i, w_in, tm=n_ctx, mod_row=bsz)
                ctx_kcol = 1
            else:
                (pc,) = _inproj(hc, mod, i, w_in[:, D_MODEL:3 * D_MODEL], tm=n_ctx, mod_row=bsz, q_chunks=0)
                ctx_kcol = 0
            h = _attn_b(p, pc, ctx_kcol, h, mod, i, w_out, _b_bias_table(b_rel_bias[j]), lng, lnb)
            if ctx_out:
                hc = _ctx_attn(pc, None, hc, mod, i, w_out, lng, lnb, mod_row=bsz, mixer_a=False)
    return h
```
